```python
import math, functools
import jax, jax.numpy as jnp
from jax import lax
import numpy as np

D_MODEL = 1024
BATCH = 2
SEQ = 16384
DEPTH = 2

GRID_W = 64
CTX_LEN = 256
EPS = 1e-6

GLA_HEADS = 4
GLA_DK = 32
GLA_DV = 64
GLA_K = GLA_HEADS * GLA_DK
GLA_V = GLA_HEADS * GLA_DV
GATE_RANK = 16
GATE_TEMP = 16.0
GLA_CHUNK = 64

DIFF_HEADS = 4
DIFF_DH = 64
DIFF_DV = 2 * DIFF_DH
DIFF_QK = DIFF_HEADS * 2 * DIFF_DH
DIFF_V = DIFF_HEADS * DIFF_DV
Q_BLOCK = 128
ROPE_BASE = 10000.0
AX_DIM = DIFF_DH // 2

POOL_WINDOWS = (2, 4, 8, 16)
POOL_CH = 64
POOL_W = len(POOL_WINDOWS) * POOL_CH

MIX_WIDTH = GLA_V + DIFF_V + POOL_W
IN_SIZES = (GLA_K, GLA_K, GLA_V, GLA_V, GATE_RANK, GATE_RANK, DIFF_QK, DIFF_QK, DIFF_V, POOL_W)
IN_COLS = GLA_K + GLA_K + GLA_V + GLA_V + GATE_RANK + GATE_RANK + DIFF_QK + DIFF_QK + DIFF_V + POOL_W

N_GROUPS = 4
EXPERTS_PER_GROUP = 4
N_EXPERTS = N_GROUPS * EXPERTS_PER_GROUP
TOP_K = 2
D_EXPERT = 512
MOE_BLOCK = 256

kernel_name = "hybrid_gla_diffattn_pool_hmoe_dit"


def rmsnorm(x, g):
    xf = x.astype(jnp.float32)
    xf = xf * lax.rsqrt(jnp.mean(xf * xf, axis=-1, keepdims=True) + EPS)
    return (xf * g.astype(jnp.float32)).astype(x.dtype)


def head_rmsnorm(o, g):
    h, d = o.shape[-2], o.shape[-1]
    of = o.astype(jnp.float32)
    of = of * lax.rsqrt(jnp.mean(of * of, axis=-1, keepdims=True) + EPS)
    return (of * g.astype(jnp.float32).reshape(h, d)).astype(o.dtype)


def adaln(cond, w_mod, b_mod):
    return jnp.split(jax.nn.silu(cond) @ w_mod + b_mod, 6, axis=-1)


def modulate(x, g, shift, scale):
    return rmsnorm(x, g) * (1.0 + scale) + shift


def axial_rope_tables(n_lat):
    rows = n_lat // GRID_W
    row = jnp.repeat(jnp.arange(rows, dtype=jnp.float32), GRID_W)
    col = jnp.tile(jnp.arange(GRID_W, dtype=jnp.float32), rows)
    inv = 1.0 / (ROPE_BASE ** (jnp.arange(0, AX_DIM, 2, dtype=jnp.float32) / AX_DIM))
    ang_r = row[:, None] * inv
    ang_c = col[:, None] * inv
    shp = (1, n_lat, 1, 1, AX_DIM // 2)
    return (jnp.cos(ang_r).reshape(shp), jnp.sin(ang_r).reshape(shp),
            jnp.cos(ang_c).reshape(shp), jnp.sin(ang_c).reshape(shp))


def rope_half(x, cos, sin):
    x1, x2 = jnp.split(x, 2, axis=-1)
    return jnp.concatenate([x1 * cos - x2 * sin, x2 * cos + x1 * sin], axis=-1)


def apply_axial_rope(x, tabs):
    cos_r, sin_r, cos_c, sin_c = tabs
    xf = x.astype(jnp.float32)
    out = jnp.concatenate([rope_half(xf[..., :AX_DIM], cos_r, sin_r),
                           rope_half(xf[..., AX_DIM:], cos_c, sin_c)], axis=-1)
    return out.astype(x.dtype)


def split_projection(z):
    parts, start = [], 0
    for size in IN_SIZES:
        parts.append(z[..., start:start + size])
        start += size
    return parts


def mixer_inputs(h, w_in, wa2_f, ba_f, wa2_b, ba_b):
    bsz, seq, _ = h.shape
    qg, kg, vg, og, af, ab, qd, kd, vd, pl = split_projection(h @ w_in)
    gla_shape = (bsz, seq, GLA_HEADS, GLA_DK)
    q_g = qg.reshape(gla_shape) * (GLA_DK ** -0.5)
    k_g = kg.reshape(gla_shape)
    v_g = vg.reshape(bsz, seq, GLA_HEADS, GLA_DV)
    la_f = jax.nn.log_sigmoid((af @ wa2_f + ba_f).astype(jnp.float32)).reshape(gla_shape) / GATE_TEMP
    la_b = jax.nn.log_sigmoid((ab @ wa2_b + ba_b).astype(jnp.float32)).reshape(gla_shape) / GATE_TEMP
    q_d = qd.reshape(bsz, seq, DIFF_HEADS, 2, DIFF_DH)
    k_d = kd.reshape(bsz, seq, DIFF_HEADS, 2, DIFF_DH)
    v_d = vd.reshape(bsz, seq, DIFF_HEADS, DIFF_DV)
    return q_g, k_g, v_g, og, la_f, la_b, q_d, k_d, v_d, pl


def gla_chunked(q, k, v, log_a, s0):
    bsz, seq, heads, _ = q.shape
    n_chunks = seq // GLA_CHUNK

    def to_chunks(t):
        return t.reshape(bsz, n_chunks, GLA_CHUNK, heads, t.shape[-1]).transpose(1, 0, 3, 2, 4)

    lower = jnp.tril(jnp.ones((GLA_CHUNK, GLA_CHUNK), dtype=bool))[:, :, None]

    def step(state, inp):
        qc, kc, vc, gc = inp
        b = jnp.cumsum(gc, axis=2)
        rel = jnp.where(lower, b[:, :, :, None, :] - b[:, :, None, :, :], -jnp.inf)
        att = jnp.einsum('bhtd,bhsd,bhtsd->bhts', qc, kc, jnp.exp(rel))
        out = (jnp.einsum('bhtd,bhde->bhte', qc * jnp.exp(b), state)
               + jnp.einsum('bhts,bhse->bhte', att, vc))
        b_end = b[:, :, -1, :]
        new_state = (jnp.exp(b_end)[..., None] * state
                     + jnp.einsum('bhsd,bhse->bhde', kc * jnp.exp(b_end[:, :, None, :] - b), vc))
        return new_state, out

    s_fin, o = lax.scan(step, s0, (to_chunks(q), to_chunks(k), to_chunks(v), to_chunks(log_a)))
    o = o.transpose(1, 0, 3, 2, 4).reshape(bsz, seq, heads, v.shape[-1])
    return o.astype(v.dtype), s_fin


def gla_output(o, gate, g):
    bsz, seq = o.shape[0], o.shape[1]
    return head_rmsnorm(o, g).reshape(bsz, seq, GLA_V) * jax.nn.silu(gate)


def diff_attend(q, k, v, lam):
    bsz, n_q, heads = q.shape[0], q.shape[1], q.shape[2]
    scale = DIFF_DH ** -0.5

    def one_block(qb):
        s = jnp.einsum('bqhmd,bkhmd->bhmqk', qb, k).astype(jnp.float32) * scale
        p = jax.nn.softmax(s, axis=-1)
        a = (p[:, :, 0] - lam * p[:, :, 1]).astype(v.dtype)
        return jnp.einsum('bhqk,bkhe->bqhe', a, v)

    n_blocks = n_q // Q_BLOCK
    qs = q.reshape(bsz, n_blocks, Q_BLOCK, heads, 2, DIFF_DH).transpose(1, 0, 2, 3, 4, 5)
    o = lax.map(one_block, qs)
    return o.transpose(1, 0, 2, 3, 4).reshape(bsz, n_q, heads, v.shape[-1])


def diff_output(o, g, lam_init):
    bsz, seq = o.shape[0], o.shape[1]
    return head_rmsnorm(o, g).reshape(bsz, seq, DIFF_V) * (1.0 - lam_init)


def multiscale_pool(p, pool_w, pool_scale):
    bsz, seq, _ = p.shape
    pf = p.astype(jnp.float32)
    cs = jnp.concatenate([jnp.zeros((bsz, 1, POOL_W), jnp.float32), jnp.cumsum(pf, axis=1)], axis=1)
    t = jnp.arange(seq)
    outs = []
    for gi, w in enumerate(POOL_WINDOWS):
        lo = jnp.clip(t - w // 2, 0, seq)
        hi = jnp.clip(t + w - w // 2, 0, seq)
        sl = slice(gi * POOL_CH, (gi + 1) * POOL_CH)
        csg = cs[..., sl]
        mean = (csg[:, hi] - csg[:, lo]) / (hi - lo).astype(jnp.float32)[None, :, None]
        outs.append(jnp.einsum('blc,cd->bld', (mean - pf[..., sl]).astype(p.dtype), pool_w[gi]))
    return jnp.concatenate(outs, axis=-1) * pool_scale


def token_mixer(h_lat, h_ctx, w_in, w_out, wa2_f, ba_f, wa2_b, ba_b, gla_norm,
                lam_q1, lam_k1, lam_q2, lam_k2, diff_norm, pool_w, pool_scale,
                lam_init, rope_tabs, with_ctx_out):
    lq_g, lk_g, lv_g, lo_g, lla_f, lla_b, lq_d, lk_d, lv_d, lpl = mixer_inputs(h_lat, w_in, wa2_f, ba_f, wa2_b, ba_b)
    cq_g, ck_g, cv_g, co_g, cla_f, cla_b, cq_d, ck_d, cv_d, cpl = mixer_inputs(h_ctx, w_in, wa2_f, ba_f, wa2_b, ba_b)
    flip = functools.partial(jnp.flip, axis=1)

    s0 = jnp.zeros((h_lat.shape[0], GLA_HEADS, GLA_DK, GLA_DV), jnp.float32)
    oc_f, s_f = gla_chunked(cq_g, ck_g, cv_g, cla_f, s0)
    oc_b, s_b = gla_chunked(flip(cq_g), flip(ck_g), flip(cv_g), flip(cla_b), s0)
    ol_f, _ = gla_chunked(lq_g, lk_g, lv_g, lla_f, s_f)
    ol_b, _ = gla_chunked(flip(lq_g), flip(lk_g), flip(lv_g), flip(lla_b), s_b)

    f32 = jnp.float32
    lam = (jnp.exp(jnp.sum(lam_q1.astype(f32) * lam_k1.astype(f32)))
           - jnp.exp(jnp.sum(lam_q2.astype(f32) * lam_k2.astype(f32))) + lam_init)
    q_rot = apply_axial_rope(lq_d, rope_tabs)
    k_rot = apply_axial_rope(lk_d, rope_tabs)
    k_all = jnp.concatenate([ck_d, k_rot], axis=1)
    v_all = jnp.concatenate([cv_d, lv_d], axis=1)

    y_lat = jnp.concatenate([
        gla_output(ol_f + flip(ol_b), lo_g, gla_norm),
        diff_output(diff_attend(q_rot, k_all, v_all, lam), diff_norm, lam_init),
        multiscale_pool(lpl, pool_w, pool_scale),
    ], axis=-1) @ w_out
    if not with_ctx_out:
        return y_lat, None
    y_ctx = jnp.concatenate([
        gla_output(oc_f + flip(oc_b), co_g, gla_norm),
        diff_output(diff_attend(cq_d, ck_d, cv_d, lam), diff_norm, lam_init),
        multiscale_pool(cpl, pool_w, pool_scale),
    ], axis=-1) @ w_out
    return y_lat, y_ctx


def hier_moe(h, wg, bg, we, be, w1, w3, w2):
    n_tok, d = h.shape
    g_prob = jax.nn.softmax((h @ wg + bg).astype(jnp.float32), axis=-1)
    g_top, g_idx = lax.top_k(g_prob, 1)
    e_logit = (h @ we + be).astype(jnp.float32).reshape(n_tok, N_GROUPS, EXPERTS_PER_GROUP)
    e_prob = jax.nn.softmax(e_logit[jnp.arange(n_tok), g_idx[:, 0]], axis=-1)
    e_top, e_loc = lax.top_k(e_prob, TOP_K)
    weights = (g_top * e_top / jnp.sum(e_top, axis=-1, keepdims=True)).reshape(-1)
    expert = (g_idx * EXPERTS_PER_GROUP + e_loc).reshape(-1)
    token = jnp.repeat(jnp.arange(n_tok, dtype=jnp.int32), TOP_K)
    n_assign = n_tok * TOP_K

    order = jnp.argsort(expert)
    e_sorted = expert[order]
    counts = jnp.zeros((N_EXPERTS,), jnp.int32).at[expert].add(1)
    padded = (counts + MOE_BLOCK - 1) // MOE_BLOCK * MOE_BLOCK
    padded_end = jnp.cumsum(padded)
    rank = jnp.arange(n_assign, dtype=jnp.int32) - (jnp.cumsum(counts) - counts)[e_sorted]
    dest = (padded_end - padded)[e_sorted] + rank
    n_slots = -(-n_assign // MOE_BLOCK) * MOE_BLOCK + N_EXPERTS * MOE_BLOCK
    n_blocks = n_slots // MOE_BLOCK
    slot_tok = jnp.full((n_slots,), n_tok, jnp.int32).at[dest].set(token[order])
    slot_w = jnp.zeros((n_slots,), h.dtype).at[dest].set(weights[order].astype(h.dtype))
    block_expert = jnp.minimum(
        jnp.searchsorted(padded_end, jnp.arange(n_blocks, dtype=jnp.int32) * MOE_BLOCK, side='right'),
        N_EXPERTS - 1)
    h_pad = jnp.concatenate([h, jnp.zeros((1, d), h.dtype)], axis=0)
    xb = h_pad[slot_tok].reshape(n_blocks, MOE_BLOCK, d)

    def expert_block(args):
        xblk, e = args
        return (jax.nn.silu(xblk @ w1[e]) * (xblk @ w3[e])) @ w2[e]

    yb = lax.map(expert_block, (xb, block_expert)).reshape(n_slots, d)
    out = jnp.zeros((n_tok + 1, d), h.dtype).at[slot_tok].add(yb * slot_w[:, None])
    return out[:n_tok]


def setup_inputs(seed: int = 0) -> dict:
    key = jax.random.key(seed)
    ks = jax.random.split(key, 32)
    f32 = jnp.float32

    def nrm(k, shape, scale):
        return jax.random.normal(k, shape, f32) * scale

    def gain(k, shape):
        return 1.0 + 0.05 * jax.random.normal(k, shape, f32)

    D = D_MODEL
    return {
        "x": nrm(ks[0], (BATCH, SEQ, D), 1.0),
        "c": nrm(ks[1], (BATCH, D), 1.0),
        "ctx": nrm(ks[2], (BATCH, CTX_LEN, D), 1.0),
        "c_ctx": nrm(ks[3], (D,), 1.0),
        "w_mod": nrm(ks[4], (DEPTH, D, 6 * D), 0.5 * D ** -0.5),
        "b_mod": nrm(ks[5], (DEPTH, 6 * D), 0.02),
        "norm1": gain(ks[6], (DEPTH, D)),
        "norm2": gain(ks[7], (DEPTH, D)),
        "w_in": nrm(ks[8], (DEPTH, D, IN_COLS), D ** -0.5),
        "w_out": nrm(ks[9], (DEPTH, MIX_WIDTH, D), MIX_WIDTH ** -0.5),
        "gla_wa2_f": nrm(ks[10], (DEPTH, GATE_RANK, GLA_K), GATE_RANK ** -0.5),
        "gla_ba_f": nrm(ks[11], (DEPTH, GLA_K), 0.1),
        "gla_wa2_b": nrm(ks[12], (DEPTH, GATE_RANK, GLA_K), GATE_RANK ** -0.5),
        "gla_ba_b": nrm(ks[13], (DEPTH, GLA_K), 0.1),
        "gla_norm": gain(ks[14], (DEPTH, GLA_V)),
        "lam_q1": nrm(ks[15], (DEPTH, DIFF_DH), 0.1),
        "lam_k1": nrm(ks[16], (DEPTH, DIFF_DH), 0.1),
        "lam_q2": nrm(ks[17], (DEPTH, DIFF_DH), 0.1),
        "lam_k2": nrm(ks[18], (DEPTH, DIFF_DH), 0.1),
        "diff_norm": gain(ks[19], (DEPTH, DIFF_V)),
        "pool_w": nrm(ks[20], (DEPTH, len(POOL_WINDOWS), POOL_CH, POOL_CH), POOL_CH ** -0.5),
        "pool_scale": gain(ks[21], (DEPTH, POOL_W)),
        "router_wg": nrm(ks[22], (DEPTH, D, N_GROUPS), D ** -0.5),
        "router_bg": nrm(ks[23], (DEPTH, N_GROUPS), 0.01),
        "router_we": nrm(ks[24], (DEPTH, D, N_EXPERTS), D ** -0.5),
        "router_be": nrm(ks[25], (DEPTH, N_EXPERTS), 0.01),
        "exp_w1": nrm(ks[26], (DEPTH, N_EXPERTS, D, D_EXPERT), D ** -0.5),
        "exp_w3": nrm(ks[27], (DEPTH, N_EXPERTS, D, D_EXPERT), D ** -0.5),
        "exp_w2": nrm(ks[28], (DEPTH, N_EXPERTS, D_EXPERT, D), D_EXPERT ** -0.5),
        "final_norm": gain(ks[29], (D,)),
    }


def reference(x, c, ctx, c_ctx, w_mod, b_mod, norm1, norm2, w_in, w_out,
              gla_wa2_f, gla_ba_f, gla_wa2_b, gla_ba_b, gla_norm,
              lam_q1, lam_k1, lam_q2, lam_k2, diff_norm, pool_w, pool_scale,
              router_wg, router_bg, router_we, router_be, exp_w1, exp_w3, exp_w2, final_norm):
    bsz, n_lat, d = x.shape
    n_ctx = ctx.shape[1]
    rope_tabs = axial_rope_tables(n_lat)
    for layer in range(DEPTH):
        last = layer == DEPTH - 1
        lam_init = 0.8 - 0.6 * math.exp(-0.3 * layer)
        sh1, sc1, g1, sh2, sc2, g2 = adaln(c[:, None, :], w_mod[layer], b_mod[layer])
        csh1, csc1, cg1, csh2, csc2, cg2 = adaln(c_ctx, w_mod[layer], b_mod[layer])

        h_lat = modulate(x, norm1[layer], sh1, sc1)
        h_ctx = modulate(ctx, norm1[layer], csh1, csc1)
        y_lat, y_ctx = token_mixer(
            h_lat, h_ctx, w_in[layer], w_out[layer],
            gla_wa2_f[layer], gla_ba_f[layer], gla_wa2_b[layer], gla_ba_b[layer], gla_norm[layer],
            lam_q1[layer], lam_k1[layer], lam_q2[layer], lam_k2[layer], diff_norm[layer],
            pool_w[layer], pool_scale[layer], lam_init, rope_tabs, not last)
        x = x + g1 * y_lat

        moe_args = (router_wg[layer], router_bg[layer], router_we[layer], router_be[layer],
                    exp_w1[layer], exp_w3[layer], exp_w2[layer])
        h2 = modulate(x, norm2[layer], sh2, sc2).reshape(-1, d)
        if last:
            x = x + g2 * hier_moe(h2, *moe_args).reshape(bsz, n_lat, d)
        else:
            ctx = ctx + cg1 * y_ctx
            h2c = modulate(ctx, norm2[layer], csh2, csc2).reshape(-1, d)
            f = hier_moe(jnp.concatenate([h2c, h2], axis=0), *moe_args)
            ctx = ctx + cg2 * f[:bsz * n_ctx].reshape(bsz, n_ctx, d)
            x = x + g2 * f[bsz * n_ctx:].reshape(bsz, n_lat, d)
    return rmsnorm(x, final_norm)
```

```python
import functools
import math

import jax
import jax.numpy as jnp
from jax import lax
from jax.experimental import pallas as pl
from jax.experimental.pallas import tpu as pltpu

F32 = jnp.float32
BF16 = jnp.bfloat16
HIGHEST = lax.Precision.HIGHEST

EPS = 1e-6
GRID_W = 64
GLA_HEADS, GLA_DK, GLA_DV = 4, 32, 64
GLA_K, GLA_V = GLA_HEADS * GLA_DK, GLA_HEADS * GLA_DV
GATE_RANK, GATE_TEMP = 16, 16.0
DIFF_HEADS, DIFF_DH = 4, 64
DIFF_DV = 2 * DIFF_DH
DIFF_QK = DIFF_HEADS * 2 * DIFF_DH
DIFF_V = DIFF_HEADS * DIFF_DV
ROPE_BASE = 10000.0
AX_DIM = DIFF_DH // 2
POOL_WINDOWS = (2, 4, 8, 16)
POOL_CH = 64
POOL_W = len(POOL_WINDOWS) * POOL_CH
POOL_HALO = 8
N_GROUPS, EXPERTS_PER_GROUP = 4, 4
N_EXPERTS = N_GROUPS * EXPERTS_PER_GROUP
TOP_K = 2
MOE_BLOCK = 256

LANES = 128
SUB = 16
ROUTE_W = LANES
VMEM_LIMIT = 56 * 1024 * 1024


def _cparams(sem):
    return pltpu.CompilerParams(dimension_semantics=sem, vmem_limit_bytes=VMEM_LIMIT)


def _dot(a, b):
    return jnp.dot(a, b, preferred_element_type=F32)


def _dot_hi(a, b):
    return jnp.dot(a, b, precision=HIGHEST, preferred_element_type=F32)


def _dot_nt(a, b):
    return lax.dot_general(a, b, (((1,), (1,)), ((), ())), preferred_element_type=F32)


def _dot_tn(a, b):
    return lax.dot_general(a, b, (((0,), (0,)), ((), ())), preferred_element_type=F32)


def _silu(x):
    return x * jax.nn.sigmoid(x)


def _log_sigmoid(x):
    return jnp.minimum(x, 0.0) - jnp.log1p(jnp.exp(-jnp.abs(x)))


def _rms(x):
    return x * lax.rsqrt(jnp.mean(x * x, axis=-1, keepdims=True) + EPS)


def _adaln_kernel(c_ref, w_ref, b_ref, o_ref):
    o_ref[0] = _dot_hi(_silu(c_ref[...]), w_ref[0]) + b_ref[0]


def _adaln(cond, w_mod, b_mod):
    depth, d, six_d = w_mod.shape
    tn = 1536
    return pl.pallas_call(
        _adaln_kernel,
        grid=(depth, six_d // tn),
        in_specs=[pl.BlockSpec((8, d), lambda l, j: (0, 0)),
                  pl.BlockSpec((1, d, tn), lambda l, j: (l, 0, j)),
                  pl.BlockSpec((1, 1, tn), lambda l, j: (l, 0, j))],
        out_specs=pl.BlockSpec((1, 8, tn), lambda l, j: (l, 0, j)),
        out_shape=jax.ShapeDtypeStruct((depth, 8, six_d), F32),
        name="adaln",
        compiler_params=_cparams(("arbitrary", "arbitrary")),
    )(cond, w_mod, b_mod.reshape(depth, 1, six_d))


_C_QG, _C_KG, _C_VG, _C_OG = 0, GLA_K, 2 * GLA_K, 2 * GLA_K + GLA_V
_C_QD = 2 * GLA_K + 2 * GLA_V
_C_KD = _C_QD + DIFF_QK
_C_VD = _C_KD + DIFF_QK
_C_PL = _C_VD + DIFF_V
_C_GT = _C_PL + POOL_W
_C_END = _C_GT + LANES


def _premix_kernel(x_ref, sh_ref, sc_ref, n1_ref, w_ref, wa2_ref, ba_ref, cos_ref, sin_ref,
                   qg_ref, kg_ref, vg_ref, og_ref, la_ref, qd_ref, kd_ref, vd_ref, pool_ref):
    x = x_ref[0]
    h = _rms(x) * n1_ref[...]
    hb = (h * (1.0 + sc_ref[0]) + sh_ref[0]).astype(BF16)

    def proj(lo, hi):
        return _dot(hb, w_ref[:, lo:hi])

    qg_ref[0] = proj(_C_QG, _C_KG) * (GLA_DK ** -0.5)
    kg_ref[0] = proj(_C_KG, _C_VG)
    vg_ref[0] = proj(_C_VG, _C_OG).astype(BF16)
    og_ref[0] = proj(_C_OG, _C_QD)
    vd_ref[0] = proj(_C_VD, _C_PL).astype(BF16)
    pool_ref[0] = proj(_C_PL, _C_GT)
    pre = _dot_hi(proj(_C_GT, _C_END), wa2_ref[...]) + ba_ref[...]
    la_ref[0] = _log_sigmoid(pre) / GATE_TEMP

    cos = cos_ref[...]
    sin = sin_ref[...]
    lane = lax.broadcasted_iota(jnp.int32, (1, LANES), 1)
    first_half = (lane % AX_DIM) < (AX_DIM // 2)

    def rope(a):
        partner = jnp.where(first_half, pltpu.roll(a, LANES - AX_DIM // 2, 1), pltpu.roll(a, AX_DIM // 2, 1))
        return a * cos + partner * sin

    for hd in range(DIFF_HEADS):
        lo = hd * LANES
        qd_ref[0, :, lo:lo + LANES] = (rope(proj(_C_QD + lo, _C_QD + lo + LANES)) * (DIFF_DH ** -0.5)).astype(BF16)
        kd_ref[0, :, lo:lo + LANES] = rope(proj(_C_KD + lo, _C_KD + lo + LANES)).astype(BF16)


def _premix(xall, mod, norm1, w_all, wa2, ba, cos_t, sin_t, nct, tm):
    bsz, seq, d = xall.shape

    def mod_row(b, i):
        return jnp.where(i < nct, bsz, b)

    tile = lambda w: pl.BlockSpec((1, tm, w), lambda b, i: (b, i, 0))
    outs = [(GLA_K, F32), (GLA_K, F32), (GLA_V, BF16), (GLA_V, F32), (2 * GLA_K, F32),
            (DIFF_QK, BF16), (DIFF_QK, BF16), (DIFF_V, BF16), (POOL_W, F32)]
    return pl.pallas_call(
        _premix_kernel,
        grid=(bsz, seq // tm),
        in_specs=[tile(d),
                  pl.BlockSpec((1, 1, d), lambda b, i: (mod_row(b, i), 0, 0)),
                  pl.BlockSpec((1, 1, d), lambda b, i: (mod_row(b, i), 0, 1)),
                  pl.BlockSpec((1, d), lambda b, i: (0, 0)),
                  pl.BlockSpec((d, _C_END), lambda b, i: (0, 0)),
                  pl.BlockSpec((LANES, 2 * GLA_K), lambda b, i: (0, 0)),
                  pl.BlockSpec((1, 2 * GLA_K), lambda b, i: (0, 0)),
                  pl.BlockSpec((tm, LANES), lambda b, i: (i, 0)),
                  pl.BlockSpec((tm, LANES), lambda b, i: (i, 0))],
        out_specs=[tile(w) for w, _ in outs],
        out_shape=[jax.ShapeDtypeStruct((bsz, seq, w), dt) for w, dt in outs],
        name="premix",
        compiler_params=_cparams(("parallel", "parallel")),
    )(xall, mod, mod, norm1, w_all, wa2, ba, cos_t, sin_t)


def _gla_kernel(qf_ref, kf_ref, vf_ref, gf_ref, qb_ref, kb_ref, vb_ref, gb_ref,
                of_ref, ob_ref, st_ref, *, bsz, rows):
    @pl.when(pl.program_id(0) == 0)
    def _():
        st_ref[...] = jnp.zeros_like(st_ref)

    n_sub = rows // SUB
    r16 = lax.broadcasted_iota(jnp.int32, (SUB, SUB), 0)
    c16 = lax.broadcasted_iota(jnp.int32, (SUB, SUB), 1)
    tri = ((c16 <= r16).astype(F32), (c16 >= r16).astype(F32))
    t_idx = lax.broadcasted_iota(jnp.int32, (SUB, 1), 0)
    same_head = (lax.broadcasted_iota(jnp.int32, (GLA_V, GLA_K), 0) // GLA_DV
                 == lax.broadcasted_iota(jnp.int32, (GLA_V, GLA_K), 1) // GLA_DK)
    expand = (lax.broadcasted_iota(jnp.int32, (GLA_K, GLA_V), 0) // GLA_DK
              == lax.broadcasted_iota(jnp.int32, (GLA_K, GLA_V), 1) // GLA_DV).astype(BF16)
    chains = [(b, 0, qf_ref, kf_ref, vf_ref, gf_ref, of_ref) for b in range(bsz)]
    chains += [(b, 1, qb_ref, kb_ref, vb_ref, gb_ref, ob_ref) for b in range(bsz)]

    def sub_chunk(j, carry):
        for b, rev, q_ref, k_ref, v_ref, g_ref, o_ref in chains:
            chain = 2 * b + rev
            jj = (n_sub - 1 - j) if rev else j
            rs = pl.ds(pl.multiple_of(jj * SUB, SUB), SUB)
            g = g_ref[b, rs, :]
            q = q_ref[b, rs, :]
            k = k_ref[b, rs, :]
            vb = v_ref[b, rs, :]
            v = vb.astype(F32)
            bloc = _dot_hi(tri[rev], g)
            bend = bloc[0:1] if rev else bloc[SUB - 1:SUB]
            st = st_ref[chain]
            o_inter = _dot_nt((q * jnp.exp(bloc)).astype(BF16), st.astype(BF16))
            qk_terms = []
            for s in range(SUB):
                valid = (t_idx <= s) if rev else (t_idx >= s)
                rel = jnp.where(valid, bloc - bloc[s:s + 1], -jnp.inf)
                qk_terms.append((q * k[s:s + 1] * jnp.exp(rel)).astype(BF16))
            att = _dot(jnp.concatenate(qk_terms, axis=0), expand)
            o_diag = att[0:SUB] * v[0:1]
            for s in range(1, SUB):
                o_diag = o_diag + att[s * SUB:(s + 1) * SUB] * v[s:s + 1]
            o_ref[b, rs, :] = o_inter + o_diag
            upd = _dot_tn(vb, (k * jnp.exp(bend - bloc)).astype(BF16))
            st_ref[chain] = st * jnp.exp(bend) + jnp.where(same_head, upd, 0.0)
        return carry

    lax.fori_loop(0, n_sub, sub_chunk, 0)


def _gla(qg, kg, vg, la, n_ctx, rows):
    bsz, seq, _ = qg.shape
    nc = n_ctx // rows
    nblk = seq // rows

    def fwd(i):
        return (0, i, 0)

    def bwd_blk(i):
        return jnp.where(i < nc, nc - 1 - i, nblk + nc - 1 - i)

    def bwd(i):
        return (0, bwd_blk(i), 0)

    def spec(w, imap):
        return pl.BlockSpec((bsz, rows, w), imap)

    return pl.pallas_call(
        functools.partial(_gla_kernel, bsz=bsz, rows=rows),
        grid=(nblk,),
        in_specs=[spec(GLA_K, fwd), spec(GLA_K, fwd), spec(GLA_V, fwd), spec(GLA_K, fwd),
                  spec(GLA_K, bwd), spec(GLA_K, bwd), spec(GLA_V, bwd),
                  spec(GLA_K, lambda i: (0, bwd_blk(i), 1))],
        out_specs=[spec(GLA_V, fwd), spec(GLA_V, bwd)],
        out_shape=[jax.ShapeDtypeStruct((bsz, seq, GLA_V), F32)] * 2,
        scratch_shapes=[pltpu.VMEM((2 * bsz, GLA_V, GLA_K), F32)],
        name="gla_scan",
        compiler_params=_cparams(("arbitrary",)),
    )(qg, kg, vg, la, qg, kg, vg, la)


def _attn_kernel(lq1_ref, lk1_ref, lq2_ref, lk2_ref, q_ref, k_ref, v_ref, o_ref,
                 m_ref, l_ref, acc_ref, *, nq_ctx, kv_ctx, kv_all, tk, lam_init):
    lam = (jnp.exp(jnp.sum(lq1_ref[...] * lk1_ref[...], axis=-1, keepdims=True))
           - jnp.exp(jnp.sum(lq2_ref[...] * lk2_ref[...], axis=-1, keepdims=True)) + lam_init)
    q = q_ref[0]
    lane = lax.broadcasted_iota(jnp.int32, (1, LANES), 1)
    zero = jnp.zeros_like(q)
    qs = (jnp.where(lane < DIFF_DH, q, zero), jnp.where(lane >= DIFF_DH, q, zero))
    m_ref[...] = jnp.full_like(m_ref, -jnp.inf)
    l_ref[...] = jnp.zeros_like(l_ref)
    acc_ref[...] = jnp.zeros_like(acc_ref)
    n_kv = jnp.where(pl.program_id(2) < nq_ctx, kv_ctx, kv_all)

    def body(j, carry):
        rs = pl.ds(pl.multiple_of(j * tk, tk), tk)
        k = k_ref[0, rs, :]
        v = v_ref[0, rs, :]
        for mp in range(2):
            s = _dot_nt(qs[mp], k)
            m_old = m_ref[mp]
            m_new = jnp.maximum(m_old, jnp.max(s, axis=-1, keepdims=True))
            p = jnp.exp(s - m_new)
            alpha = jnp.exp(m_old - m_new)
            l_ref[mp] = alpha * l_ref[mp] + jnp.sum(p, axis=-1, keepdims=True)
            acc_ref[mp] = alpha * acc_ref[mp] + _dot(p.astype(BF16), v)
            m_ref[mp] = m_new
        return carry

    lax.fori_loop(0, n_kv, body, 0)
    o_ref[0] = acc_ref[0] / l_ref[0] - lam * (acc_ref[1] / l_ref[1])


def _attention(qd, kd, vd, lam_vecs, lam_init, n_ctx, tq, tk):
    bsz, seq, _ = qd.shape
    lam_spec = pl.BlockSpec((1, DIFF_DH), lambda b, h, i: (0, 0))
    return pl.pallas_call(
        functools.partial(_attn_kernel, nq_ctx=n_ctx // tq, kv_ctx=n_ctx // tk, kv_all=seq // tk, tk=tk,
                          lam_init=lam_init),
        grid=(bsz, DIFF_HEADS, seq // tq),
        in_specs=[lam_spec] * 4 + [
            pl.BlockSpec((1, tq, LANES), lambda b, h, i: (b, i, h)),
            pl.BlockSpec((1, seq, LANES), lambda b, h, i: (b, 0, h)),
            pl.BlockSpec((1, seq, LANES), lambda b, h, i: (b, 0, h))],
        out_specs=pl.BlockSpec((1, tq, LANES), lambda b, h, i: (b, i, h)),
        out_shape=jax.ShapeDtypeStruct((bsz, seq, DIFF_V), F32),
        scratch_shapes=[pltpu.VMEM((2, tq, 1), F32), pltpu.VMEM((2, tq, 1), F32),
                        pltpu.VMEM((2, tq, LANES), F32)],
        name="diff_attn",
        compiler_params=_cparams(("parallel", "parallel", "arbitrary")),
    )(*lam_vecs, qd, kd, vd)


def _postmix_kernel(x_ref, of_ref, ob_ref, og_ref, od_ref, pc_ref, pp_ref, pn_ref,
                    gn_ref, dn_ref, pw_ref, ps_ref, wo_ref, g1_ref, sh2_ref, sc2_ref, n2_ref,
                    wr_ref, br_ref, x1_ref, h2_ref, route_ref, ext_ref,
                    *, tm, nct, n_ctx, seq, lam_init):
    i = pl.program_id(1)

    a = of_ref[0] + ob_ref[0]
    avg = (lax.broadcasted_iota(jnp.int32, (GLA_V, GLA_V), 0) // GLA_DV
           == lax.broadcasted_iota(jnp.int32, (GLA_V, GLA_V), 1) // GLA_DV).astype(F32) * (1.0 / GLA_DV)
    gla = a * lax.rsqrt(_dot_hi(a * a, avg) + EPS) * gn_ref[...] * _silu(og_ref[0])
    y = _dot(gla.astype(BF16), wo_ref[0:GLA_V, :])

    for hd in range(DIFF_HEADS):
        lo = hd * DIFF_DV
        dh = _rms(od_ref[0, :, lo:lo + DIFF_DV]) * dn_ref[:, lo:lo + DIFF_DV] * (1.0 - lam_init)
        y = y + _dot(dh.astype(BF16), wo_ref[GLA_V + lo:GLA_V + lo + DIFF_DV, :])

    seg_lo = jnp.where(i < nct, 0, n_ctx)
    seg_hi = jnp.where(i < nct, n_ctx, seq)
    ext_ref[0:POOL_HALO] = pp_ref[0]
    ext_ref[POOL_HALO:POOL_HALO + tm] = pc_ref[0]
    ext_ref[POOL_HALO + tm:] = pn_ref[0]
    pos_e = i * tm - POOL_HALO + lax.broadcasted_iota(jnp.int32, (tm + 2 * POOL_HALO, 1), 0)
    e = jnp.where((pos_e >= seg_lo) & (pos_e < seg_hi), ext_ref[...], 0.0)
    pos = i * tm + lax.broadcasted_iota(jnp.int32, (tm, 1), 0)
    grp = lax.broadcasted_iota(jnp.int32, (1, POOL_W), 1) // POOL_CH
    run, width, mean = e, 1, jnp.zeros((tm, POOL_W), F32)
    for gi, w in enumerate(POOL_WINDOWS):
        while width < w:
            n = run.shape[0] - width
            run = run[0:n] + run[width:width + n]
            width *= 2
        start = POOL_HALO - w // 2
        cnt = (jnp.minimum(pos + (w - w // 2), seg_hi) - jnp.maximum(pos - w // 2, seg_lo)).astype(F32)
        mean = jnp.where(grp == gi, run[start:start + tm] / cnt, mean)
    pooled = _dot((mean - pc_ref[0]).astype(BF16), pw_ref[...]) * ps_ref[...]
    y = y + _dot(pooled.astype(BF16), wo_ref[GLA_V + DIFF_V:, :])

    x1 = x_ref[0] + g1_ref[0] * y
    x1_ref[0] = x1
    h2 = _rms(x1) * n2_ref[...] * (1.0 + sc2_ref[0]) + sh2_ref[0]
    h2_ref[0] = h2

    logit = _dot_hi(h2, wr_ref[...]) + br_ref[...]
    lane = lax.broadcasted_iota(jnp.int32, (1, ROUTE_W), 1).astype(F32)
    neg = -jnp.inf

    def top(vals):
        mx = jnp.max(vals, axis=-1, keepdims=True)
        idx = jnp.min(jnp.where(vals == mx, lane, float(ROUTE_W)), axis=-1, keepdims=True)
        return mx, idx

    gl = jnp.where(lane < N_GROUPS, logit, neg)
    gmax, gidx = top(gl)
    g_top = 1.0 / jnp.sum(jnp.exp(gl - gmax), axis=-1, keepdims=True)
    e_lo = N_GROUPS + gidx * EXPERTS_PER_GROUP
    el = jnp.where((lane >= e_lo) & (lane < e_lo + EXPERTS_PER_GROUP), logit, neg)
    emax, idx1 = top(el)
    esum = jnp.sum(jnp.exp(el - emax), axis=-1, keepdims=True)
    emax2, idx2 = top(jnp.where(lane == idx1, neg, el))
    e1 = 1.0 / esum
    e2 = jnp.exp(emax2 - emax) / esum
    w1 = g_top * e1 / (e1 + e2)
    w2 = g_top * e2 / (e1 + e2)
    rec = jnp.where(lane == 0, idx1 - N_GROUPS, 0.0)
    rec = jnp.where(lane == 1, idx2 - N_GROUPS, rec)
    rec = jnp.where(lane == 2, w1, rec)
    route_ref[0] = jnp.where(lane == 3, w2, rec)


def _postmix(xall, o_f, o_b, og, od, pool, mod, prm, nct, n_ctx, tm, lam_init):
    bsz, seq, d = xall.shape
    hpb = tm // POOL_HALO
    n_halo = seq // POOL_HALO

    def mod_spec(col):
        return pl.BlockSpec((1, 1, d), lambda b, i: (jnp.where(i < nct, bsz, b), 0, col))

    tile = lambda w: pl.BlockSpec((1, tm, w), lambda b, i: (b, i, 0))
    full = lambda r, c: pl.BlockSpec((r, c), lambda b, i: (0, 0))
    return pl.pallas_call(
        functools.partial(_postmix_kernel, tm=tm, nct=nct, n_ctx=n_ctx, seq=seq, lam_init=lam_init),
        grid=(bsz, seq // tm),
        in_specs=[tile(d), tile(GLA_V), tile(GLA_V), tile(GLA_V), tile(DIFF_V), tile(POOL_W),
                  pl.BlockSpec((1, POOL_HALO, POOL_W), lambda b, i: (b, jnp.maximum(i * hpb - 1, 0), 0)),
                  pl.BlockSpec((1, POOL_HALO, POOL_W), lambda b, i: (b, jnp.minimum((i + 1) * hpb, n_halo - 1), 0)),
                  full(1, GLA_V), full(1, DIFF_V), full(POOL_W, POOL_W), full(1, POOL_W), full(d, d),
                  mod_spec(2), mod_spec(3), mod_spec(4), full(1, d), full(d, ROUTE_W), full(1, ROUTE_W)],
        out_specs=[tile(d), tile(d), tile(ROUTE_W)],
        out_shape=[jax.ShapeDtypeStruct((bsz, seq, d), F32), jax.ShapeDtypeStruct((bsz, seq, d), F32),
                   jax.ShapeDtypeStruct((bsz, seq, ROUTE_W), F32)],
        scratch_shapes=[pltpu.VMEM((tm + 2 * POOL_HALO, POOL_W), F32)],
        name="postmix",
        compiler_params=_cparams(("parallel", "parallel")),
    )(xall, o_f, o_b, og, od, pool, pool, pool, prm["gla_norm"], prm["diff_norm"], prm["pool_w"],
      prm["pool_scale"], prm["w_out"], mod, mod, mod, prm["norm2"], prm["w_route"], prm["b_route"])


def _moe_kernel(be_ref, nused_ref, src_ref, dst_ref, h_hbm, w1_ref, w3_ref, w2_ref, y_hbm,
                xbuf, ybuf, gsem, ssem):
    del be_ref

    def gather(r):
        return pltpu.make_async_copy(h_hbm.at[pl.ds(src_ref[0, 0, r], 1), :], xbuf.at[pl.ds(r, 1), :], gsem)

    def scatter(r):
        return pltpu.make_async_copy(ybuf.at[pl.ds(r, 1), :], y_hbm.at[pl.ds(dst_ref[0, 0, r], 1), :], ssem)

    def each_row(fn):
        def step(r, carry):
            fn(r)
            return carry
        lax.fori_loop(0, MOE_BLOCK, step, 0)

    def each_valid_row(fn):
        def guarded(r):
            @pl.when(dst_ref[0, 0, r] >= 0)
            def _():
                fn(r)
        each_row(guarded)

    @pl.when(pl.program_id(0) < nused_ref[0])
    def _():
        each_row(lambda r: gather(r).start())
        each_row(lambda r: gather(r).wait())
        x = xbuf[...].astype(BF16)
        mid = (_silu(_dot(x, w1_ref[0])) * _dot(x, w3_ref[0])).astype(BF16)
        ybuf[...] = _dot(mid, w2_ref[0])
        each_valid_row(lambda r: scatter(r).start())
        each_valid_row(lambda r: scatter(r).wait())


def _moe(h2, block_expert, n_used, slot_src, slot_dst, w1, w3, w2):
    n_tok, d = h2.shape
    n_blocks = block_expert.shape[0]
    d_exp = w1.shape[-1]
    slot_spec = pl.BlockSpec((1, 1, MOE_BLOCK), lambda i, be, nu: (i, 0, 0), memory_space=pltpu.SMEM)
    return pl.pallas_call(
        _moe_kernel,
        grid_spec=pltpu.PrefetchScalarGridSpec(
            num_scalar_prefetch=2,
            grid=(n_blocks,),
            in_specs=[slot_spec, slot_spec,
                      pl.BlockSpec(memory_space=pl.ANY),
                      pl.BlockSpec((1, d, d_exp), lambda i, be, nu: (be[i], 0, 0)),
                      pl.BlockSpec((1, d, d_exp), lambda i, be, nu: (be[i], 0, 0)),
                      pl.BlockSpec((1, d_exp, d), lambda i, be, nu: (be[i], 0, 0))],
            out_specs=pl.BlockSpec(memory_space=pl.ANY),
            scratch_shapes=[pltpu.VMEM((MOE_BLOCK, d), F32), pltpu.VMEM((MOE_BLOCK, d), F32),
                            pltpu.SemaphoreType.DMA(()), pltpu.SemaphoreType.DMA(())]),
        out_shape=jax.ShapeDtypeStruct((TOP_K * n_tok, d), F32),
        name="moe_ffn",
        compiler_params=_cparams(("arbitrary",)),
    )(block_expert, n_used, slot_src.reshape(n_blocks, 1, MOE_BLOCK), slot_dst.reshape(n_blocks, 1, MOE_BLOCK),
      h2, w1, w3, w2)


def _slot_tables(expert, n_tok):
    n_assign = n_tok * TOP_K
    e = expert.reshape(n_assign)
    hot = (e[:, None] == jnp.arange(N_EXPERTS, dtype=jnp.int32)[None, :]).astype(jnp.int32)
    csum = jnp.cumsum(hot, axis=0)
    counts = csum[-1]
    rank = jnp.sum(csum * hot, axis=1) - 1
    padded = (counts + MOE_BLOCK - 1) // MOE_BLOCK * MOE_BLOCK
    padded_end = jnp.cumsum(padded)
    dest = (padded_end - padded)[e] + rank
    n_slots = -(-n_assign // MOE_BLOCK) * MOE_BLOCK + N_EXPERTS * MOE_BLOCK
    n_blocks = n_slots // MOE_BLOCK
    assign = jnp.arange(n_assign, dtype=jnp.int32)
    slot_src = jnp.zeros((n_slots,), jnp.int32).at[dest].set(assign // TOP_K)
    slot_dst = jnp.full((n_slots,), -1, jnp.int32).at[dest].set((assign % TOP_K) * n_tok + assign // TOP_K)
    block_expert = jnp.minimum(
        jnp.searchsorted(padded_end, jnp.arange(n_blocks, dtype=jnp.int32) * MOE_BLOCK, side="right"),
        N_EXPERTS - 1).astype(jnp.int32)
    n_used = (padded_end[-1:] // MOE_BLOCK).astype(jnp.int32)
    return block_expert, n_used, slot_src, slot_dst


def _combine_kernel(x1_ref, y0_ref, y1_ref, route_ref, g2_ref, fn_ref, o_ref, *, final):
    route = route_ref[0]
    lane = lax.broadcasted_iota(jnp.int32, (1, ROUTE_W), 1)
    w0 = jnp.sum(jnp.where(lane == 2, route, 0.0), axis=-1, keepdims=True)
    w1 = jnp.sum(jnp.where(lane == 3, route, 0.0), axis=-1, keepdims=True)
    x = x1_ref[0] + g2_ref[0] * (y0_ref[0, 0] * w0 + y1_ref[0, 0] * w1)
    o_ref[0] = _rms(x) * fn_ref[...] if final else x


def _combine(x1, y2, route, mod, final_norm, nct, tm, final):
    bsz, seq, d = x1.shape
    off = nct if final else 0
    tile = lambda w: pl.BlockSpec((1, tm, w), lambda b, i: (b, i + off, 0))
    ysel = lambda c: pl.BlockSpec((1, 1, tm, d), lambda b, i: (c, b, i + off, 0))
    return pl.pallas_call(
        functools.partial(_combine_kernel, final=final),
        grid=(bsz, seq // tm - off),
        in_specs=[tile(d), ysel(0), ysel(1), tile(ROUTE_W),
                  pl.BlockSpec((1, 1, d), lambda b, i: (jnp.where(i + off < nct, bsz, b), 0, 5)),
                  pl.BlockSpec((1, d), lambda b, i: (0, 0))],
        out_specs=pl.BlockSpec((1, tm, d), lambda b, i: (b, i, 0)),
        out_shape=jax.ShapeDtypeStruct((bsz, seq - off * tm, d), F32),
        name="combine",
        compiler_params=_cparams(("parallel", "parallel")),
    )(x1, y2.reshape(TOP_K, bsz, seq, d), y2.reshape(TOP_K, bsz, seq, d), route, mod, final_norm)


def _rope_tables(n_ctx, n_lat):
    t = jnp.arange(n_lat, dtype=jnp.int32)
    row = (t // GRID_W).astype(F32)
    col = (t % GRID_W).astype(F32)
    inv = 1.0 / (ROPE_BASE ** (jnp.arange(0, AX_DIM, 2, dtype=F32) / AX_DIM))
    lane = jnp.arange(LANES)
    within = lane % DIFF_DH
    pos = jnp.where((within < AX_DIM)[None, :], row[:, None], col[:, None])
    ang = pos * inv[within % (AX_DIM // 2)][None, :]
    sign = jnp.where((within % AX_DIM) < AX_DIM // 2, -1.0, 1.0)[None, :]
    cos = jnp.concatenate([jnp.ones((n_ctx, LANES), F32), jnp.cos(ang)], axis=0)
    sin = jnp.concatenate([jnp.zeros((n_ctx, LANES), F32), jnp.sin(ang) * sign], axis=0)
    return cos, sin


def _pack_layer(layer, w_in, w_out, wa2_f, ba_f, wa2_b, ba_b, pool_w, wg, bg, we, be):
    d = w_in.shape[1]
    wi = w_in[layer]
    o = 0
    parts = {}
    for name, size in (("qg", GLA_K), ("kg", GLA_K), ("vg", GLA_V), ("og", GLA_V), ("af", GATE_RANK),
                       ("ab", GATE_RANK), ("qd", DIFF_QK), ("kd", DIFF_QK), ("vd", DIFF_V), ("pl", POOL_W)):
        parts[name] = wi[:, o:o + size]
        o += size
    gate = jnp.concatenate([parts["af"], parts["ab"], jnp.zeros((d, LANES - 2 * GATE_RANK), F32)], axis=1)
    w_all = jnp.concatenate([parts[n] for n in ("qg", "kg", "vg", "og", "qd", "kd", "vd", "pl")] + [gate],
                            axis=1).astype(BF16)
    wa2 = jnp.zeros((LANES, 2 * GLA_K), F32)
    wa2 = wa2.at[0:GATE_RANK, 0:GLA_K].set(wa2_f[layer])
    wa2 = wa2.at[GATE_RANK:2 * GATE_RANK, GLA_K:].set(wa2_b[layer])
    ba = jnp.concatenate([ba_f[layer], ba_b[layer]])[None, :]
    pw = jnp.zeros((POOL_W, POOL_W), F32)
    for gi in range(len(POOL_WINDOWS)):
        pw = pw.at[gi * POOL_CH:(gi + 1) * POOL_CH, gi * POOL_CH:(gi + 1) * POOL_CH].set(pool_w[layer, gi])
    w_route = jnp.concatenate([wg[layer], we[layer], jnp.zeros((d, ROUTE_W - N_GROUPS - N_EXPERTS), F32)], axis=1)
    b_route = jnp.concatenate([bg[layer], be[layer], jnp.zeros((ROUTE_W - N_GROUPS - N_EXPERTS,), F32)])[None, :]
    return dict(w_all=w_all, wa2=wa2, ba=ba, pool_w=pw.astype(BF16), w_out=w_out[layer].astype(BF16),
                w_route=w_route, b_route=b_route)


def kernel(x, c, ctx, c_ctx, w_mod, b_mod, norm1, norm2, w_in, w_out, gla_wa2_f, gla_ba_f, gla_wa2_b, gla_ba_b, gla_norm, lam_q1, lam_k1, lam_q2, lam_k2, diff_norm, pool_w, pool_scale, router_wg, router_bg, router_we, router_be, exp_w1, exp_w3, exp_w2, final_norm):
    bsz, n_lat, d = x.shape
    n_ctx = ctx.shape[1]
    depth = w_mod.shape[0]
    seq = n_ctx + n_lat
    tm = math.gcd(256, n_ctx)
    nct = n_ctx // tm
    tq = tk = tm
    gla_rows = math.gcd(128, n_ctx)
    assert bsz + 1 <= 8 and n_lat % tm == 0 and n_lat % GRID_W == 0

    cond = jnp.concatenate([c, c_ctx[None, :], jnp.zeros((8 - bsz - 1, d), F32)], axis=0)
    mod_all = _adaln(cond, w_mod, b_mod)
    cos_t, sin_t = _rope_tables(n_ctx, n_lat)
    xall = jnp.concatenate([ctx, x], axis=1)
    n_tok = bsz * seq

    for layer in range(depth):
        last = layer == depth - 1
        lam_init = 0.8 - 0.6 * math.exp(-0.3 * layer)
        prm = _pack_layer(layer, w_in, w_out, gla_wa2_f, gla_ba_f, gla_wa2_b, gla_ba_b, pool_w,
                          router_wg, router_bg, router_we, router_be)
        prm.update(gla_norm=gla_norm[layer][None, :], diff_norm=diff_norm[layer][None, :],
                   pool_scale=pool_scale[layer][None, :], norm2=norm2[layer][None, :])
        mod = mod_all[layer].reshape(8, 1, 6 * d)

        qg, kg, vg, og, la, qd, kd, vd, pool = _premix(
            xall, mod, norm1[layer][None, :], prm["w_all"], prm["wa2"], prm["ba"], cos_t, sin_t, nct, tm)
        o_f, o_b = _gla(qg, kg, vg, la, n_ctx, gla_rows)
        lam_vecs = [v[layer][None, :] for v in (lam_q1, lam_k1, lam_q2, lam_k2)]
        od = _attention(qd, kd, vd, lam_vecs, lam_init, n_ctx, tq, tk)
        x1, h2, route = _postmix(xall, o_f, o_b, og, od, pool, mod, prm, nct, n_ctx, tm, lam_init)

        expert = route[..., 0:TOP_K].astype(jnp.int32).reshape(n_tok, TOP_K)
        block_expert, n_used, slot_src, slot_dst = _slot_tables(expert, n_tok)
        y2 = _moe(h2.reshape(n_tok, d), block_expert, n_used, slot_src, slot_dst,
                  exp_w1[layer].astype(BF16), exp_w3[layer].astype(BF16), exp_w2[layer].astype(BF16))
        xall = _combine(x1, y2, route, mod, final_norm[None, :], nct, tm, last)
    return xall
```

```python
import functools
import math

import jax
import jax.numpy as jnp
from jax import lax
from jax.experimental import pallas as pl
from jax.experimental.pallas import tpu as pltpu

F32 = jnp.float32
BF16 = jnp.bfloat16
HIGHEST = lax.Precision.HIGHEST

EPS = 1e-6
GRID_W = 64
GLA_HEADS, GLA_DK, GLA_DV = 4, 32, 64
GLA_K, GLA_V = GLA_HEADS * GLA_DK, GLA_HEADS * GLA_DV
GATE_RANK, GATE_TEMP = 16, 16.0
DIFF_HEADS, DIFF_DH = 4, 64
DIFF_DV = 2 * DIFF_DH
DIFF_QK = DIFF_HEADS * 2 * DIFF_DH
DIFF_V = DIFF_HEADS * DIFF_DV
ROPE_BASE = 10000.0
AX_DIM = DIFF_DH // 2
POOL_WINDOWS = (2, 4, 8, 16)
POOL_CH = 64
POOL_W = len(POOL_WINDOWS) * POOL_CH
POOL_HALO = 8
N_GROUPS, EXPERTS_PER_GROUP = 4, 4
N_EXPERTS = N_GROUPS * EXPERTS_PER_GROUP
TOP_K = 2
MOE_BLOCK = 256

LANES = 128
SUB = 16
ROUTE_W = LANES
VMEM_LIMIT = 56 * 1024 * 1024


def _cparams(sem):
    return pltpu.CompilerParams(dimension_semantics=sem, vmem_limit_bytes=VMEM_LIMIT)


def _dot(a, b):
    return jnp.dot(a, b, preferred_element_type=F32)


def _dot_hi(a, b):
    return jnp.dot(a, b, precision=HIGHEST, preferred_element_type=F32)


def _dot_nt(a, b):
    return lax.dot_general(a, b, (((1,), (1,)), ((), ())), preferred_element_type=F32)


def _dot_tn(a, b):
    return lax.dot_general(a, b, (((0,), (0,)), ((), ())), preferred_element_type=F32)


def _silu(x):
    return x * jax.nn.sigmoid(x)


def _log_sigmoid(x):
    return jnp.minimum(x, 0.0) - jnp.log1p(jnp.exp(-jnp.abs(x)))


def _rms(x):
    return x * lax.rsqrt(jnp.mean(x * x, axis=-1, keepdims=True) + EPS)


def _adaln_kernel(c_ref, w_ref, b_ref, o_ref):
    o_ref[0] = _dot_hi(_silu(c_ref[...]), w_ref[0]) + b_ref[0]


def _adaln(cond, w_mod, b_mod):
    depth, d, six_d = w_mod.shape
    tn = 1536
    return pl.pallas_call(
        _adaln_kernel,
        grid=(depth, six_d // tn),
        in_specs=[pl.BlockSpec((8, d), lambda l, j: (0, 0)),
                  pl.BlockSpec((1, d, tn), lambda l, j: (l, 0, j)),
                  pl.BlockSpec((1, 1, tn), lambda l, j: (l, 0, j))],
        out_specs=pl.BlockSpec((1, 8, tn), lambda l, j: (l, 0, j)),
        out_shape=jax.ShapeDtypeStruct((depth, 8, six_d), F32),
        name="adaln",
        compiler_params=_cparams(("arbitrary", "arbitrary")),
    )(cond, w_mod, b_mod.reshape(depth, 1, six_d))


_C_QG, _C_KG, _C_VG, _C_OG = 0, GLA_K, 2 * GLA_K, 2 * GLA_K + GLA_V
_C_QD = 2 * GLA_K + 2 * GLA_V
_C_KD = _C_QD + DIFF_QK
_C_VD = _C_KD + DIFF_QK
_C_PL = _C_VD + DIFF_V
_C_GT = _C_PL + POOL_W
_C_END = _C_GT + LANES


def _premix_kernel(x_ref, sh_ref, sc_ref, n1_ref, w_ref, wa2_ref, ba_ref, cos_ref, sin_ref,
                   qg_ref, kg_ref, vg_ref, og_ref, la_ref, qd_ref, kd_ref, vd_ref, pool_ref):
    x = x_ref[0]
    h = _rms(x) * n1_ref[...]
    hb = (h * (1.0 + sc_ref[0]) + sh_ref[0]).astype(BF16)

    def proj(lo, hi):
        return _dot(hb, w_ref[:, lo:hi])

    qg_ref[0] = proj(_C_QG, _C_KG) * (GLA_DK ** -0.5)
    kg_ref[0] = proj(_C_KG, _C_VG)
    vg_ref[0] = proj(_C_VG, _C_OG).astype(BF16)
    og_ref[0] = proj(_C_OG, _C_QD)
    vd_ref[0] = proj(_C_VD, _C_PL).astype(BF16)
    pool_ref[0] = proj(_C_PL, _C_GT)
    pre = _dot_hi(proj(_C_GT, _C_END), wa2_ref[...]) + ba_ref[...]
    la_ref[0] = _log_sigmoid(pre) / GATE_TEMP

    cos = cos_ref[...]
    sin = sin_ref[...]
    lane = lax.broadcasted_iota(jnp.int32, (1, LANES), 1)
    first_half = (lane % AX_DIM) < (AX_DIM // 2)

    def rope(a):
        partner = jnp.where(first_half, pltpu.roll(a, LANES - AX_DIM // 2, 1), pltpu.roll(a, AX_DIM // 2, 1))
        return a * cos + partner * sin

    for hd in range(DIFF_HEADS):
        lo = hd * LANES
        qd_ref[0, :, lo:lo + LANES] = (rope(proj(_C_QD + lo, _C_QD + lo + LANES)) * (DIFF_DH ** -0.5)).astype(BF16)
        kd_ref[0, :, lo:lo + LANES] = rope(proj(_C_KD + lo, _C_KD + lo + LANES)).astype(BF16)


def _premix(xall, mod, norm1, w_all, wa2, ba, cos_t, sin_t, nct, tm):
    bsz, seq, d = xall.shape

    def mod_row(b, i):
        return jnp.where(i < nct, bsz, b)

    tile = lambda w: pl.BlockSpec((1, tm, w), lambda b, i: (b, i, 0))
    outs = [(GLA_K, F32), (GLA_K, F32), (GLA_V, BF16), (GLA_V, F32), (2 * GLA_K, F32),
            (DIFF_QK, BF16), (DIFF_QK, BF16), (DIFF_V, BF16), (POOL_W, F32)]
    return pl.pallas_call(
        _premix_kernel,
        grid=(bsz, seq // tm),
        in_specs=[tile(d),
                  pl.BlockSpec((1, 1, d), lambda b, i: (mod_row(b, i), 0, 0)),
                  pl.BlockSpec((1, 1, d), lambda b, i: (mod_row(b, i), 0, 1)),
                  pl.BlockSpec((1, d), lambda b, i: (0, 0)),
                  pl.BlockSpec((d, _C_END), lambda b, i: (0, 0)),
                  pl.BlockSpec((LANES, 2 * GLA_K), lambda b, i: (0, 0)),
                  pl.BlockSpec((1, 2 * GLA_K), lambda b, i: (0, 0)),
                  pl.BlockSpec((tm, LANES), lambda b, i: (i, 0)),
                  pl.BlockSpec((tm, LANES), lambda b, i: (i, 0))],
        out_specs=[tile(w) for w, _ in outs],
        out_shape=[jax.ShapeDtypeStruct((bsz, seq, w), dt) for w, dt in outs],
        name="premix",
        compiler_params=_cparams(("parallel", "parallel")),
    )(xall, mod, mod, norm1, w_all, wa2, ba, cos_t, sin_t)


def _gla_kernel(qf_ref, kf_ref, vf_ref, gf_ref, qb_ref, kb_ref, vb_ref, gb_ref,
                of_ref, ob_ref, st_ref, *, bsz, rows):
    @pl.when(pl.program_id(0) == 0)
    def _():
        st_ref[...] = jnp.zeros_like(st_ref)

    n_sub = rows // SUB
    r16 = lax.broadcasted_iota(jnp.int32, (SUB, SUB), 0)
    c16 = lax.broadcasted_iota(jnp.int32, (SUB, SUB), 1)
    tri = ((c16 <= r16).astype(F32), (c16 >= r16).astype(F32))
    t_idx = lax.broadcasted_iota(jnp.int32, (SUB, 1), 0)
    same_head = (lax.broadcasted_iota(jnp.int32, (GLA_V, GLA_K), 0) // GLA_DV
                 == lax.broadcasted_iota(jnp.int32, (GLA_V, GLA_K), 1) // GLA_DK)
    expand = (lax.broadcasted_iota(jnp.int32, (GLA_K, GLA_V), 0) // GLA_DK
              == lax.broadcasted_iota(jnp.int32, (GLA_K, GLA_V), 1) // GLA_DV).astype(BF16)
    chains = [(b, 0, qf_ref, kf_ref, vf_ref, gf_ref, of_ref) for b in range(bsz)]
    chains += [(b, 1, qb_ref, kb_ref, vb_ref, gb_ref, ob_ref) for b in range(bsz)]

    def sub_chunk(j, carry):
        for b, rev, q_ref, k_ref, v_ref, g_ref, o_ref in chains:
            chain = 2 * b + rev
            jj = (n_sub - 1 - j) if rev else j
            rs = pl.ds(pl.multiple_of(jj * SUB, SUB), SUB)
            g = g_ref[b, rs, :]
            q = q_ref[b, rs, :]
            k = k_ref[b, rs, :]
            vb = v_ref[b, rs, :]
            v = vb.astype(F32)
            bloc = _dot_hi(tri[rev], g)
            bend = bloc[0:1] if rev else bloc[SUB - 1:SUB]
            st = st_ref[chain]
            o_inter = _dot_nt((q * jnp.exp(bloc)).astype(BF16), st.astype(BF16))
            qk_terms = []
            for s in range(SUB):
                valid = (t_idx <= s) if rev else (t_idx >= s)
                rel = jnp.where(valid, bloc - bloc[s:s + 1], -jnp.inf)
                qk_terms.append((q * k[s:s + 1] * jnp.exp(rel)).astype(BF16))
            att = _dot(jnp.concatenate(qk_terms, axis=0), expand)
            o_diag = att[0:SUB] * v[0:1]
            for s in range(1, SUB):
                o_diag = o_diag + att[s * SUB:(s + 1) * SUB] * v[s:s + 1]
            o_ref[b, rs, :] = o_inter + o_diag
            upd = _dot_tn(vb, (k * jnp.exp(bend - bloc)).astype(BF16))
            st_ref[chain] = st * jnp.exp(bend) + jnp.where(same_head, upd, 0.0)
        return carry

    lax.fori_loop(0, n_sub, sub_chunk, 0)


def _gla(qg, kg, vg, la, n_ctx, rows):
    bsz, seq, _ = qg.shape
    nc = n_ctx // rows
    nblk = seq // rows

    def fwd(i):
        return (0, i, 0)

    def bwd_blk(i):
        return jnp.where(i < nc, nc - 1 - i, nblk + nc - 1 - i)

    def bwd(i):
        return (0, bwd_blk(i), 0)

    def spec(w, imap):
        return pl.BlockSpec((bsz, rows, w), imap)

    return pl.pallas_call(
        functools.partial(_gla_kernel, bsz=bsz, rows=rows),
        grid=(nblk,),
        in_specs=[spec(GLA_K, fwd), spec(GLA_K, fwd), spec(GLA_V, fwd), spec(GLA_K, fwd),
                  spec(GLA_K, bwd), spec(GLA_K, bwd), spec(GLA_V, bwd),
                  spec(GLA_K, lambda i: (0, bwd_blk(i), 1))],
        out_specs=[spec(GLA_V, fwd), spec(GLA_V, bwd)],
        out_shape=[jax.ShapeDtypeStruct((bsz, seq, GLA_V), F32)] * 2,
        scratch_shapes=[pltpu.VMEM((2 * bsz, GLA_V, GLA_K), F32)],
        name="gla_scan",
        compiler_params=_cparams(("arbitrary",)),
    )(qg, kg, vg, la, qg, kg, vg, la)


def _attn_kernel(lq1_ref, lk1_ref, lq2_ref, lk2_ref, q_ref, k_ref, v_ref, o_ref,
                 m_ref, acc_ref, sa_ref, sb_ref, *, n_ctx, n_chunks, tk, lam_init):
    lam = (jnp.exp(jnp.sum(lq1_ref[...] * lk1_ref[...], axis=-1, keepdims=True))
           - jnp.exp(jnp.sum(lq2_ref[...] * lk2_ref[...], axis=-1, keepdims=True)) + lam_init)
    q = q_ref[0]
    lane = lax.broadcasted_iota(jnp.int32, (1, LANES), 1)
    zero = jnp.zeros_like(q)
    qs = (jnp.where(lane < DIFF_DH, q, zero), jnp.where(lane >= DIFF_DH, q, zero))
    m_ref[...] = jnp.full_like(m_ref, -jnp.inf)
    acc_ref[...] = jnp.zeros_like(acc_ref)

    def scores(start, size, s_ref):
        k = k_ref[0, pl.ds(start, size), :]
        for mp in range(2):
            s_ref[mp, :, 0:size] = _dot_nt(qs[mp], k)

    def consume(start, size, s_ref):
        v_ext = jnp.concatenate([v_ref[0, pl.ds(start, size), :], jnp.ones((size, LANES), BF16)], axis=1)
        for mp in range(2):
            s = s_ref[mp, :, 0:size]
            m_old = m_ref[mp]
            m_new = jnp.maximum(m_old, jnp.max(s, axis=-1, keepdims=True))
            p = jnp.exp(s - m_new).astype(BF16)
            acc_ref[mp] = jnp.exp(m_old - m_new) * acc_ref[mp] + _dot(p, v_ext)
            m_ref[mp] = m_new

    def latent(j):
        return pl.multiple_of(n_ctx + j * tk, math.gcd(n_ctx, tk))

    scores(0, n_ctx, sa_ref)
    if n_chunks == 0:
        consume(0, n_ctx, sa_ref)
    else:
        scores(latent(0), tk, sb_ref)
        consume(0, n_ctx, sa_ref)

        def chunk_pair(jj, carry):
            scores(latent(2 * jj + 1), tk, sa_ref)
            consume(latent(2 * jj), tk, sb_ref)
            scores(latent(jnp.minimum(2 * jj + 2, n_chunks - 1)), tk, sb_ref)
            consume(latent(2 * jj + 1), tk, sa_ref)
            return carry

        lax.fori_loop(0, n_chunks // 2, chunk_pair, 0)
    o_ref[0] = (acc_ref[0, :, 0:LANES] / acc_ref[0, :, LANES:]
                - lam * (acc_ref[1, :, 0:LANES] / acc_ref[1, :, LANES:]))


def _attention(qd, kd, vd, lam_vecs, lam_init, q_lo, n_q, n_ctx, n_kv, tq, tk):
    bsz = qd.shape[0]
    n_chunks = (n_kv - n_ctx) // tk
    assert n_chunks % 2 == 0 and n_ctx + n_chunks * tk == n_kv and q_lo % tq == 0 and n_q % tq == 0
    lam_spec = pl.BlockSpec((1, DIFF_DH), lambda b, h, i: (0, 0))
    return pl.pallas_call(
        functools.partial(_attn_kernel, n_ctx=n_ctx, n_chunks=n_chunks, tk=tk, lam_init=lam_init),
        grid=(bsz, DIFF_HEADS, n_q // tq),
        in_specs=[lam_spec] * 4 + [
            pl.BlockSpec((1, tq, LANES), lambda b, h, i: (b, i + q_lo // tq, h)),
            pl.BlockSpec((1, n_kv, LANES), lambda b, h, i: (b, 0, h)),
            pl.BlockSpec((1, n_kv, LANES), lambda b, h, i: (b, 0, h))],
        out_specs=pl.BlockSpec((1, tq, LANES), lambda b, h, i: (b, i, h)),
        out_shape=jax.ShapeDtypeStruct((bsz, n_q, DIFF_V), F32),
        scratch_shapes=[pltpu.VMEM((2, tq, 1), F32), pltpu.VMEM((2, tq, 2 * LANES), F32),
                        pltpu.VMEM((2, tq, max(tk, n_ctx)), F32), pltpu.VMEM((2, tq, max(tk, n_ctx)), F32)],
        name="diff_attn",
        compiler_params=_cparams(("parallel", "parallel", "arbitrary")),
    )(*lam_vecs, qd, kd, vd)


def _postmix_kernel(x_ref, of_ref, ob_ref, og_ref, odc_ref, odl_ref, pc_ref, pp_ref, pn_ref,
                    gn_ref, dn_ref, pw_ref, ps_ref, wo_ref, g1_ref, sh2_ref, sc2_ref, n2_ref,
                    wr_ref, br_ref, x1_ref, h2_ref, route_ref, ext_ref,
                    *, tm, nct, n_ctx, seq, lam_init):
    i = pl.program_id(1)

    a = of_ref[0] + ob_ref[0]
    avg = (lax.broadcasted_iota(jnp.int32, (GLA_V, GLA_V), 0) // GLA_DV
           == lax.broadcasted_iota(jnp.int32, (GLA_V, GLA_V), 1) // GLA_DV).astype(F32) * (1.0 / GLA_DV)
    gla = a * lax.rsqrt(_dot_hi(a * a, avg) + EPS) * gn_ref[...] * _silu(og_ref[0])
    y = _dot(gla.astype(BF16), wo_ref[0:GLA_V, :])

    for hd in range(DIFF_HEADS):
        lo = hd * DIFF_DV
        od = jnp.where(i < nct, odc_ref[0, :, lo:lo + DIFF_DV], odl_ref[0, :, lo:lo + DIFF_DV])
        dh = _rms(od) * dn_ref[:, lo:lo + DIFF_DV] * (1.0 - lam_init)
        y = y + _dot(dh.astype(BF16), wo_ref[GLA_V + lo:GLA_V + lo + DIFF_DV, :])

    seg_lo = jnp.where(i < nct, 0, n_ctx)
    seg_hi = jnp.where(i < nct, n_ctx, seq)
    ext_ref[0:POOL_HALO] = pp_ref[0]
    ext_ref[POOL_HALO:POOL_HALO + tm] = pc_ref[0]
    ext_ref[POOL_HALO + tm:] = pn_ref[0]
    pos_e = i * tm - POOL_HALO + lax.broadcasted_iota(jnp.int32, (tm + 2 * POOL_HALO, 1), 0)
    e = jnp.where((pos_e >= seg_lo) & (pos_e < seg_hi), ext_ref[...], 0.0)
    pos = i * tm + lax.broadcasted_iota(jnp.int32, (tm, 1), 0)
    grp = lax.broadcasted_iota(jnp.int32, (1, POOL_W), 1) // POOL_CH
    run, width, mean = e, 1, jnp.zeros((tm, POOL_W), F32)
    for gi, w in enumerate(POOL_WINDOWS):
        while width < w:
            n = run.shape[0] - width
            run = run[0:n] + run[width:width + n]
            width *= 2
        start = POOL_HALO - w // 2
        cnt = (jnp.minimum(pos + (w - w // 2), seg_hi) - jnp.maximum(pos - w // 2, seg_lo)).astype(F32)
        mean = jnp.where(grp == gi, run[start:start + tm] / cnt, mean)
    pooled = _dot((mean - pc_ref[0]).astype(BF16), pw_ref[...]) * ps_ref[...]
    y = y + _dot(pooled.astype(BF16), wo_ref[GLA_V + DIFF_V:, :])

    x1 = x_ref[0] + g1_ref[0] * y
    x1_ref[0] = x1
    h2 = _rms(x1) * n2_ref[...] * (1.0 + sc2_ref[0]) + sh2_ref[0]
    h2_ref[0] = h2

    logit = _dot_hi(h2, wr_ref[...]) + br_ref[...]
    lane = lax.broadcasted_iota(jnp.int32, (1, ROUTE_W), 1).astype(F32)
    neg = -jnp.inf

    def top(vals):
        mx = jnp.max(vals, axis=-1, keepdims=True)
        idx = jnp.min(jnp.where(vals == mx, lane, float(ROUTE_W)), axis=-1, keepdims=True)
        return mx, idx

    gl = jnp.where(lane < N_GROUPS, logit, neg)
    gmax, gidx = top(gl)
    g_top = 1.0 / jnp.sum(jnp.exp(gl - gmax), axis=-1, keepdims=True)
    e_lo = N_GROUPS + gidx * EXPERTS_PER_GROUP
    el = jnp.where((lane >= e_lo) & (lane < e_lo + EXPERTS_PER_GROUP), logit, neg)
    emax, idx1 = top(el)
    esum = jnp.sum(jnp.exp(el - emax), axis=-1, keepdims=True)
    emax2, idx2 = top(jnp.where(lane == idx1, neg, el))
    e1 = 1.0 / esum
    e2 = jnp.exp(emax2 - emax) / esum
    w1 = g_top * e1 / (e1 + e2)
    w2 = g_top * e2 / (e1 + e2)
    rec = jnp.where(lane == 0, idx1 - N_GROUPS, 0.0)
    rec = jnp.where(lane == 1, idx2 - N_GROUPS, rec)
    rec = jnp.where(lane == 2, w1, rec)
    route_ref[0] = jnp.where(lane == 3, w2, rec)


def _postmix(xall, o_f, o_b, og, od_ctx, od_lat, pool, mod, prm, nct, n_ctx, tm, lam_init):
    bsz, seq, d = xall.shape
    hpb = tm // POOL_HALO
    n_halo = seq // POOL_HALO

    def mod_spec(col):
        return pl.BlockSpec((1, 1, d), lambda b, i: (jnp.where(i < nct, bsz, b), 0, col))

    tile = lambda w: pl.BlockSpec((1, tm, w), lambda b, i: (b, i, 0))
    full = lambda r, c: pl.BlockSpec((r, c), lambda b, i: (0, 0))
    return pl.pallas_call(
        functools.partial(_postmix_kernel, tm=tm, nct=nct, n_ctx=n_ctx, seq=seq, lam_init=lam_init),
        grid=(bsz, seq // tm),
        in_specs=[tile(d), tile(GLA_V), tile(GLA_V), tile(GLA_V),
                  pl.BlockSpec((1, tm, DIFF_V), lambda b, i: (b, jnp.minimum(i, nct - 1), 0)),
                  pl.BlockSpec((1, tm, DIFF_V), lambda b, i: (b, jnp.maximum(i - nct, 0), 0)),
                  tile(POOL_W),
                  pl.BlockSpec((1, POOL_HALO, POOL_W), lambda b, i: (b, jnp.maximum(i * hpb - 1, 0), 0)),
                  pl.BlockSpec((1, POOL_HALO, POOL_W), lambda b, i: (b, jnp.minimum((i + 1) * hpb, n_halo - 1), 0)),
                  full(1, GLA_V), full(1, DIFF_V), full(POOL_W, POOL_W), full(1, POOL_W), full(d, d),
                  mod_spec(2), mod_spec(3), mod_spec(4), full(1, d), full(d, ROUTE_W), full(1, ROUTE_W)],
        out_specs=[tile(d), tile(d), tile(ROUTE_W)],
        out_shape=[jax.ShapeDtypeStruct((bsz, seq, d), F32), jax.ShapeDtypeStruct((bsz, seq, d), F32),
                   jax.ShapeDtypeStruct((bsz, seq, ROUTE_W), F32)],
        scratch_shapes=[pltpu.VMEM((tm + 2 * POOL_HALO, POOL_W), F32)],
        name="postmix",
        compiler_params=_cparams(("parallel", "parallel")),
    )(xall, o_f, o_b, og, od_ctx, od_lat, pool, pool, pool, prm["gla_norm"], prm["diff_norm"], prm["pool_w"],
      prm["pool_scale"], prm["w_out"], mod, mod, mod, prm["norm2"], prm["w_route"], prm["b_route"])


def _moe_kernel(be_ref, nused_ref, src_ref, dst_ref, h_hbm, w1_ref, w3_ref, w2_ref, y_hbm,
                xbuf, ybuf, gsem, ssem):
    del be_ref

    def gather(r):
        return pltpu.make_async_copy(h_hbm.at[pl.ds(src_ref[0, 0, r], 1), :], xbuf.at[pl.ds(r, 1), :], gsem)

    def scatter(r):
        return pltpu.make_async_copy(ybuf.at[pl.ds(r, 1), :], y_hbm.at[pl.ds(dst_ref[0, 0, r], 1), :], ssem)

    def each_row(fn):
        def step(r, carry):
            fn(r)
            return carry
        lax.fori_loop(0, MOE_BLOCK, step, 0, unroll=8)

    def each_valid_row(fn):
        def guarded(r):
            @pl.when(dst_ref[0, 0, r] >= 0)
            def _():
                fn(r)
        each_row(guarded)

    @pl.when(pl.program_id(0) < nused_ref[0])
    def _():
        each_row(lambda r: gather(r).start())
        each_row(lambda r: gather(r).wait())
        x = xbuf[...].astype(BF16)
        mid = (_silu(_dot(x, w1_ref[0])) * _dot(x, w3_ref[0])).astype(BF16)
        ybuf[...] = _dot(mid, w2_ref[0])
        each_valid_row(lambda r: scatter(r).start())
        each_valid_row(lambda r: scatter(r).wait())


def _moe(h2, block_expert, n_used, slot_src, slot_dst, w1, w3, w2):
    n_tok, d = h2.shape
    n_blocks = block_expert.shape[0]
    d_exp = w1.shape[-1]
    slot_spec = pl.BlockSpec((1, 1, MOE_BLOCK), lambda i, be, nu: (i, 0, 0), memory_space=pltpu.SMEM)
    return pl.pallas_call(
        _moe_kernel,
        grid_spec=pltpu.PrefetchScalarGridSpec(
            num_scalar_prefetch=2,
            grid=(n_blocks,),
            in_specs=[slot_spec, slot_spec,
                      pl.BlockSpec(memory_space=pl.ANY),
                      pl.BlockSpec((1, d, d_exp), lambda i, be, nu: (be[i], 0, 0)),
                      pl.BlockSpec((1, d, d_exp), lambda i, be, nu: (be[i], 0, 0)),
                      pl.BlockSpec((1, d_exp, d), lambda i, be, nu: (be[i], 0, 0))],
            out_specs=pl.BlockSpec(memory_space=pl.ANY),
            scratch_shapes=[pltpu.VMEM((MOE_BLOCK, d), F32), pltpu.VMEM((MOE_BLOCK, d), F32),
                            pltpu.SemaphoreType.DMA(()), pltpu.SemaphoreType.DMA(())]),
        out_shape=jax.ShapeDtypeStruct((TOP_K * n_tok, d), F32),
        name="moe_ffn",
        compiler_params=_cparams(("arbitrary",)),
    )(block_expert, n_used, slot_src.reshape(n_blocks, 1, MOE_BLOCK), slot_dst.reshape(n_blocks, 1, MOE_BLOCK),
      h2, w1, w3, w2)


def _slot_tables(expert, n_tok):
    n_assign = n_tok * TOP_K
    e = expert.reshape(n_assign)
    hot = (e[:, None] == jnp.arange(N_EXPERTS, dtype=jnp.int32)[None, :]).astype(jnp.int32)
    csum = jnp.cumsum(hot, axis=0)
    counts = csum[-1]
    rank = jnp.sum(csum * hot, axis=1) - 1
    padded = (counts + MOE_BLOCK - 1) // MOE_BLOCK * MOE_BLOCK
    padded_end = jnp.cumsum(padded)
    dest = (padded_end - padded)[e] + rank
    n_slots = -(-n_assign // MOE_BLOCK) * MOE_BLOCK + N_EXPERTS * MOE_BLOCK
    n_blocks = n_slots // MOE_BLOCK
    assign = jnp.arange(n_assign, dtype=jnp.int32)
    slot_src = jnp.zeros((n_slots,), jnp.int32).at[dest].set(assign // TOP_K)
    slot_dst = jnp.full((n_slots,), -1, jnp.int32).at[dest].set((assign % TOP_K) * n_tok + assign // TOP_K)
    block_start = jnp.arange(n_blocks, dtype=jnp.int32) * MOE_BLOCK
    block_expert = jnp.minimum(jnp.sum((padded_end[None, :] <= block_start[:, None]).astype(jnp.int32), axis=1),
                               N_EXPERTS - 1)
    n_used = (padded_end[-1:] // MOE_BLOCK).astype(jnp.int32)
    return block_expert, n_used, slot_src, slot_dst


def _combine_kernel(x1_ref, y0_ref, y1_ref, route_ref, g2_ref, fn_ref, o_ref, *, final):
    route = route_ref[0]
    lane = lax.broadcasted_iota(jnp.int32, (1, ROUTE_W), 1)
    w0 = jnp.sum(jnp.where(lane == 2, route, 0.0), axis=-1, keepdims=True)
    w1 = jnp.sum(jnp.where(lane == 3, route, 0.0), axis=-1, keepdims=True)
    x = x1_ref[0] + g2_ref[0] * (y0_ref[0, 0] * w0 + y1_ref[0, 0] * w1)
    o_ref[0] = _rms(x) * fn_ref[...] if final else x


def _combine(x1, y2, route, mod, final_norm, nct, tm, final):
    bsz, seq, d = x1.shape
    off = nct if final else 0
    tile = lambda w: pl.BlockSpec((1, tm, w), lambda b, i: (b, i + off, 0))
    ysel = lambda c: pl.BlockSpec((1, 1, tm, d), lambda b, i: (c, b, i + off, 0))
    return pl.pallas_call(
        functools.partial(_combine_kernel, final=final),
        grid=(bsz, seq // tm - off),
        in_specs=[tile(d), ysel(0), ysel(1), tile(ROUTE_W),
                  pl.BlockSpec((1, 1, d), lambda b, i: (jnp.where(i + off < nct, bsz, b), 0, 5)),
                  pl.BlockSpec((1, d), lambda b, i: (0, 0))],
        out_specs=pl.BlockSpec((1, tm, d), lambda b, i: (b, i, 0)),
        out_shape=jax.ShapeDtypeStruct((bsz, seq - off * tm, d), F32),
        name="combine",
        compiler_params=_cparams(("parallel", "parallel")),
    )(x1, y2.reshape(TOP_K, bsz, seq, d), y2.reshape(TOP_K, bsz, seq, d), route, mod, final_norm)


def _rope_tables(n_ctx, n_lat):
    t = jnp.arange(n_lat, dtype=jnp.int32)
    row = (t // GRID_W).astype(F32)
    col = (t % GRID_W).astype(F32)
    inv = 1.0 / (ROPE_BASE ** (jnp.arange(0, AX_DIM, 2, dtype=F32) / AX_DIM))
    lane = jnp.arange(LANES)
    within = lane % DIFF_DH
    pos = jnp.where((within < AX_DIM)[None, :], row[:, None], col[:, None])
    ang = pos * inv[within % (AX_DIM // 2)][None, :]
    sign = jnp.where((within % AX_DIM) < AX_DIM // 2, -1.0, 1.0)[None, :]
    cos = jnp.concatenate([jnp.ones((n_ctx, LANES), F32), jnp.cos(ang)], axis=0)
    sin = jnp.concatenate([jnp.zeros((n_ctx, LANES), F32), jnp.sin(ang) * sign], axis=0)
    return cos, sin


def _pack_layer(layer, w_in, w_out, wa2_f, ba_f, wa2_b, ba_b, pool_w, wg, bg, we, be):
    d = w_in.shape[1]
    wi = w_in[layer]
    o = 0
    parts = {}
    for name, size in (("qg", GLA_K), ("kg", GLA_K), ("vg", GLA_V), ("og", GLA_V), ("af", GATE_RANK),
                       ("ab", GATE_RANK), ("qd", DIFF_QK), ("kd", DIFF_QK), ("vd", DIFF_V), ("pl", POOL_W)):
        parts[name] = wi[:, o:o + size]
        o += size
    gate = jnp.concatenate([parts["af"], parts["ab"], jnp.zeros((d, LANES - 2 * GATE_RANK), F32)], axis=1)
    w_all = jnp.concatenate([parts[n] for n in ("qg", "kg", "vg", "og", "qd", "kd", "vd", "pl")] + [gate],
                            axis=1).astype(BF16)
    wa2 = jnp.zeros((LANES, 2 * GLA_K), F32)
    wa2 = wa2.at[0:GATE_RANK, 0:GLA_K].set(wa2_f[layer])
    wa2 = wa2.at[GATE_RANK:2 * GATE_RANK, GLA_K:].set(wa2_b[layer])
    ba = jnp.concatenate([ba_f[layer], ba_b[layer]])[None, :]
    pw = jnp.zeros((POOL_W, POOL_W), F32)
    for gi in range(len(POOL_WINDOWS)):
        pw = pw.at[gi * POOL_CH:(gi + 1) * POOL_CH, gi * POOL_CH:(gi + 1) * POOL_CH].set(pool_w[layer, gi])
    w_route = jnp.concatenate([wg[layer], we[layer], jnp.zeros((d, ROUTE_W - N_GROUPS - N_EXPERTS), F32)], axis=1)
    b_route = jnp.concatenate([bg[layer], be[layer], jnp.zeros((ROUTE_W - N_GROUPS - N_EXPERTS,), F32)])[None, :]
    return dict(w_all=w_all, wa2=wa2, ba=ba, pool_w=pw.astype(BF16), w_out=w_out[layer].astype(BF16),
                w_route=w_route, b_route=b_route)


def kernel(x, c, ctx, c_ctx, w_mod, b_mod, norm1, norm2, w_in, w_out, gla_wa2_f, gla_ba_f, gla_wa2_b, gla_ba_b, gla_norm, lam_q1, lam_k1, lam_q2, lam_k2, diff_norm, pool_w, pool_scale, router_wg, router_bg, router_we, router_be, exp_w1, exp_w3, exp_w2, final_norm):
    bsz, n_lat, d = x.shape
    n_ctx = ctx.shape[1]
    depth = w_mod.shape[0]
    seq = n_ctx + n_lat
    tm = math.gcd(256, n_ctx)
    nct = n_ctx // tm
    tq = tm
    tk = math.gcd(1024, n_lat // 2)
    gla_rows = math.gcd(128, n_ctx)
    assert bsz + 1 <= 8 and n_lat % tm == 0 and n_lat % GRID_W == 0

    cond = jnp.concatenate([c, c_ctx[None, :], jnp.zeros((8 - bsz - 1, d), F32)], axis=0)
    mod_all = _adaln(cond, w_mod, b_mod)
    cos_t, sin_t = _rope_tables(n_ctx, n_lat)
    xall = jnp.concatenate([ctx, x], axis=1)
    n_tok = bsz * seq

    for layer in range(depth):
        last = layer == depth - 1
        lam_init = 0.8 - 0.6 * math.exp(-0.3 * layer)
        prm = _pack_layer(layer, w_in, w_out, gla_wa2_f, gla_ba_f, gla_wa2_b, gla_ba_b, pool_w,
                          router_wg, router_bg, router_we, router_be)
        prm.update(gla_norm=gla_norm[layer][None, :], diff_norm=diff_norm[layer][None, :],
                   pool_scale=pool_scale[layer][None, :], norm2=norm2[layer][None, :])
        mod = mod_all[layer].reshape(8, 1, 6 * d)

        qg, kg, vg, og, la, qd, kd, vd, pool = _premix(
            xall, mod, norm1[layer][None, :], prm["w_all"], prm["wa2"], prm["ba"], cos_t, sin_t, nct, tm)
        o_f, o_b = _gla(qg, kg, vg, la, n_ctx, gla_rows)
        lam_vecs = [v[layer][None, :] for v in (lam_q1, lam_k1, lam_q2, lam_k2)]
        od_lat = _attention(qd, kd, vd, lam_vecs, lam_init, n_ctx, n_lat, n_ctx, seq, tq, tk)
        od_ctx = _attention(qd, kd, vd, lam_vecs, lam_init, 0, n_ctx, n_ctx, n_ctx, tq, tk)
        x1, h2, route = _postmix(xall, o_f, o_b, og, od_ctx, od_lat, pool, mod, prm, nct, n_ctx, tm, lam_init)

        expert = route[..., 0:TOP_K].astype(jnp.int32).reshape(n_tok, TOP_K)
        block_expert, n_used, slot_src, slot_dst = _slot_tables(expert, n_tok)
        y2 = _moe(h2.reshape(n_tok, d), block_expert, n_used, slot_src, slot_dst,
                  exp_w1[layer].astype(BF16), exp_w3[layer].astype(BF16), exp_w2[layer].astype(BF16))
        xall = _combine(x1, y2, route, mod, final_norm[None, :], nct, tm, last)
    return xall
```

```python
import functools
import math

import jax
import jax.numpy as jnp
from jax import lax
from jax.experimental import pallas as pl
from jax.experimental.pallas import tpu as pltpu

F32 = jnp.float32
BF16 = jnp.bfloat16
HIGHEST = lax.Precision.HIGHEST

EPS = 1e-6
GRID_W = 64
GLA_HEADS, GLA_DK, GLA_DV = 4, 32, 64
GLA_K, GLA_V = GLA_HEADS * GLA_DK, GLA_HEADS * GLA_DV
GATE_RANK, GATE_TEMP = 16, 16.0
DIFF_HEADS, DIFF_DH = 4, 64
DIFF_DV = 2 * DIFF_DH
DIFF_QK = DIFF_HEADS * 2 * DIFF_DH
DIFF_V = DIFF_HEADS * DIFF_DV
ROPE_BASE = 10000.0
AX_DIM = DIFF_DH // 2
POOL_WINDOWS = (2, 4, 8, 16)
POOL_CH = 64
POOL_W = len(POOL_WINDOWS) * POOL_CH
POOL_HALO = 8
N_GROUPS, EXPERTS_PER_GROUP = 4, 4
N_EXPERTS = N_GROUPS * EXPERTS_PER_GROUP
TOP_K = 2
MOE_BLOCK = 256

LANES = 128
SUB = 16
ROUTE_W = LANES
VMEM_LIMIT = 56 * 1024 * 1024


def _cparams(sem):
    return pltpu.CompilerParams(dimension_semantics=sem, vmem_limit_bytes=VMEM_LIMIT)


def _dot(a, b):
    return jnp.dot(a, b, preferred_element_type=F32)


def _dot_hi(a, b):
    return jnp.dot(a, b, precision=HIGHEST, preferred_element_type=F32)


def _dot_nt(a, b):
    return lax.dot_general(a, b, (((1,), (1,)), ((), ())), preferred_element_type=F32)


def _dot_tn(a, b):
    return lax.dot_general(a, b, (((0,), (0,)), ((), ())), preferred_element_type=F32)


def _silu(x):
    return x * jax.nn.sigmoid(x)


def _log_sigmoid(x):
    return jnp.minimum(x, 0.0) - jnp.log1p(jnp.exp(-jnp.abs(x)))


def _rms(x):
    return x * lax.rsqrt(jnp.mean(x * x, axis=-1, keepdims=True) + EPS)


def _adaln_kernel(c_ref, w_ref, b_ref, o_ref):
    o_ref[0] = _dot_hi(_silu(c_ref[...]), w_ref[0]) + b_ref[0]


def _adaln(cond, w_mod, b_mod):
    depth, d, six_d = w_mod.shape
    tn = 1536
    return pl.pallas_call(
        _adaln_kernel,
        grid=(depth, six_d // tn),
        in_specs=[pl.BlockSpec((8, d), lambda l, j: (0, 0)),
                  pl.BlockSpec((1, d, tn), lambda l, j: (l, 0, j)),
                  pl.BlockSpec((1, 1, tn), lambda l, j: (l, 0, j))],
        out_specs=pl.BlockSpec((1, 8, tn), lambda l, j: (l, 0, j)),
        out_shape=jax.ShapeDtypeStruct((depth, 8, six_d), F32),
        name="adaln",
        compiler_params=_cparams(("arbitrary", "arbitrary")),
    )(cond, w_mod, b_mod.reshape(depth, 1, six_d))


_C_QG, _C_KG, _C_VG, _C_OG = 0, GLA_K, 2 * GLA_K, 2 * GLA_K + GLA_V
_C_QD = 2 * GLA_K + 2 * GLA_V
_C_KD = _C_QD + DIFF_QK
_C_VD = _C_KD + DIFF_QK
_C_PL = _C_VD + DIFF_V
_C_GT = _C_PL + POOL_W
_C_END = _C_GT + LANES


def _premix_kernel(x_ref, sh_ref, sc_ref, n1_ref, w_ref, wa2_ref, ba_ref, cos_ref, sin_ref,
                   qg_ref, kg_ref, vg_ref, og_ref, la_ref, qd_ref, kd_ref, vd_ref, pool_ref):
    x = x_ref[0]
    h = _rms(x) * n1_ref[...]
    hb = (h * (1.0 + sc_ref[0]) + sh_ref[0]).astype(BF16)

    def proj(lo, hi):
        return _dot(hb, w_ref[:, lo:hi])

    qg_ref[0] = proj(_C_QG, _C_KG) * (GLA_DK ** -0.5)
    kg_ref[0] = proj(_C_KG, _C_VG)
    vg_ref[0] = proj(_C_VG, _C_OG).astype(BF16)
    og_ref[0] = proj(_C_OG, _C_QD)
    vd_ref[0] = proj(_C_VD, _C_PL).astype(BF16)
    pool_ref[0] = proj(_C_PL, _C_GT)
    pre = _dot_hi(proj(_C_GT, _C_END), wa2_ref[...]) + ba_ref[...]
    la_ref[0] = _log_sigmoid(pre) / GATE_TEMP

    cos = cos_ref[...]
    sin = sin_ref[...]
    lane = lax.broadcasted_iota(jnp.int32, (1, LANES), 1)
    first_half = (lane % AX_DIM) < (AX_DIM // 2)

    def rope(a):
        partner = jnp.where(first_half, pltpu.roll(a, LANES - AX_DIM // 2, 1), pltpu.roll(a, AX_DIM // 2, 1))
        return a * cos + partner * sin

    for hd in range(DIFF_HEADS):
        lo = hd * LANES
        qd_ref[0, :, lo:lo + LANES] = (rope(proj(_C_QD + lo, _C_QD + lo + LANES)) * (DIFF_DH ** -0.5)).astype(BF16)
        kd_ref[0, :, lo:lo + LANES] = rope(proj(_C_KD + lo, _C_KD + lo + LANES)).astype(BF16)


def _premix(xall, mod, norm1, w_all, wa2, ba, cos_t, sin_t, nct, tm):
    bsz, seq, d = xall.shape

    def mod_row(b, i):
        return jnp.where(i < nct, bsz, b)

    tile = lambda w: pl.BlockSpec((1, tm, w), lambda b, i: (b, i, 0))
    outs = [(GLA_K, F32), (GLA_K, F32), (GLA_V, BF16), (GLA_V, F32), (2 * GLA_K, F32),
            (DIFF_QK, BF16), (DIFF_QK, BF16), (DIFF_V, BF16), (POOL_W, F32)]
    return pl.pallas_call(
        _premix_kernel,
        grid=(bsz, seq // tm),
        in_specs=[tile(d),
                  pl.BlockSpec((1, 1, d), lambda b, i: (mod_row(b, i), 0, 0)),
                  pl.BlockSpec((1, 1, d), lambda b, i: (mod_row(b, i), 0, 1)),
                  pl.BlockSpec((1, d), lambda b, i: (0, 0)),
                  pl.BlockSpec((d, _C_END), lambda b, i: (0, 0)),
                  pl.BlockSpec((LANES, 2 * GLA_K), lambda b, i: (0, 0)),
                  pl.BlockSpec((1, 2 * GLA_K), lambda b, i: (0, 0)),
                  pl.BlockSpec((tm, LANES), lambda b, i: (i, 0)),
                  pl.BlockSpec((tm, LANES), lambda b, i: (i, 0))],
        out_specs=[tile(w) for w, _ in outs],
        out_shape=[jax.ShapeDtypeStruct((bsz, seq, w), dt) for w, dt in outs],
        name="premix",
        compiler_params=_cparams(("parallel", "parallel")),
    )(xall, mod, mod, norm1, w_all, wa2, ba, cos_t, sin_t)


def _gla_kernel(qf_ref, kf_ref, vf_ref, gf_ref, qb_ref, kb_ref, vb_ref, gb_ref,
                of_ref, ob_ref, st_ref, *, bsz, rows):
    @pl.when(pl.program_id(0) == 0)
    def _():
        st_ref[...] = jnp.zeros_like(st_ref)

    n_sub = rows // SUB
    rr = lax.broadcasted_iota(jnp.int32, (rows, rows), 0)
    cc = lax.broadcasted_iota(jnp.int32, (rows, rows), 1)
    same_sub = rr // SUB == cc // SUB
    tri = ((same_sub & (cc <= rr)).astype(F32), (same_sub & (cc >= rr)).astype(F32))
    t_idx = lax.broadcasted_iota(jnp.int32, (SUB, 1), 0)
    same_head = (lax.broadcasted_iota(jnp.int32, (GLA_V, GLA_K), 0) // GLA_DV
                 == lax.broadcasted_iota(jnp.int32, (GLA_V, GLA_K), 1) // GLA_DK)
    expand = (lax.broadcasted_iota(jnp.int32, (GLA_K, GLA_V), 0) // GLA_DK
              == lax.broadcasted_iota(jnp.int32, (GLA_K, GLA_V), 1) // GLA_DV).astype(BF16)
    chains = [(b, 0, qf_ref, kf_ref, vf_ref, gf_ref, of_ref) for b in range(bsz)]
    chains += [(b, 1, qb_ref, kb_ref, vb_ref, gb_ref, ob_ref) for b in range(bsz)]

    for b, rev, q_ref, k_ref, v_ref, g_ref, o_ref in chains:
        bloc_all = _dot_hi(tri[rev], g_ref[b])
        order = range(n_sub - 1, -1, -1) if rev else range(n_sub)
        parts = {}
        for j in order:
            rs = slice(j * SUB, (j + 1) * SUB)
            q = q_ref[b, rs, :]
            k = k_ref[b, rs, :]
            vb = v_ref[b, rs, :]
            v = vb.astype(F32)
            bloc = bloc_all[rs]
            bend = bloc[0:1] if rev else bloc[SUB - 1:SUB]
            qk_terms = []
            for s in range(SUB):
                valid = (t_idx <= s) if rev else (t_idx >= s)
                rel = jnp.where(valid, bloc - bloc[s:s + 1], -jnp.inf)
                qk_terms.append((q * k[s:s + 1] * jnp.exp(rel)).astype(BF16))
            att = _dot(jnp.concatenate(qk_terms, axis=0), expand)
            o_diag = att[0:SUB] * v[0:1]
            for s in range(1, SUB):
                o_diag = o_diag + att[s * SUB:(s + 1) * SUB] * v[s:s + 1]
            upd = _dot_tn(vb, (k * jnp.exp(bend - bloc)).astype(BF16))
            parts[j] = ((q * jnp.exp(bloc)).astype(BF16), o_diag, jnp.exp(bend), jnp.where(same_head, upd, 0.0))
        st = st_ref[2 * b + rev]
        for j in order:
            q_dec, o_diag, decay, upd = parts[j]
            o_ref[b, j * SUB:(j + 1) * SUB, :] = _dot_nt(q_dec, st.astype(BF16)) + o_diag
            st = st * decay + upd
        st_ref[2 * b + rev] = st


def _gla(qg, kg, vg, la, n_ctx, rows):
    bsz, seq, _ = qg.shape
    nc = n_ctx // rows
    nblk = seq // rows

    def fwd(i):
        return (0, i, 0)

    def bwd_blk(i):
        return jnp.where(i < nc, nc - 1 - i, nblk + nc - 1 - i)

    def bwd(i):
        return (0, bwd_blk(i), 0)

    def spec(w, imap):
        return pl.BlockSpec((bsz, rows, w), imap)

    return pl.pallas_call(
        functools.partial(_gla_kernel, bsz=bsz, rows=rows),
        grid=(nblk,),
        in_specs=[spec(GLA_K, fwd), spec(GLA_K, fwd), spec(GLA_V, fwd), spec(GLA_K, fwd),
                  spec(GLA_K, bwd), spec(GLA_K, bwd), spec(GLA_V, bwd),
                  spec(GLA_K, lambda i: (0, bwd_blk(i), 1))],
        out_specs=[spec(GLA_V, fwd), spec(GLA_V, bwd)],
        out_shape=[jax.ShapeDtypeStruct((bsz, seq, GLA_V), F32)] * 2,
        scratch_shapes=[pltpu.VMEM((2 * bsz, GLA_V, GLA_K), F32)],
        name="gla_scan",
        compiler_params=_cparams(("arbitrary",)),
    )(qg, kg, vg, la, qg, kg, vg, la)


def _attn_kernel(lq1_ref, lk1_ref, lq2_ref, lk2_ref, q_ref, k_ref, v_ref, o_ref,
                 m_ref, acc_ref, sa_ref, sb_ref, *, n_ctx, n_chunks, tk, lam_init):
    lam = (jnp.exp(jnp.sum(lq1_ref[...] * lk1_ref[...], axis=-1, keepdims=True))
           - jnp.exp(jnp.sum(lq2_ref[...] * lk2_ref[...], axis=-1, keepdims=True)) + lam_init)
    q = q_ref[0]
    lane = lax.broadcasted_iota(jnp.int32, (1, LANES), 1)
    zero = jnp.zeros_like(q)
    qs = (jnp.where(lane < DIFF_DH, q, zero), jnp.where(lane >= DIFF_DH, q, zero))
    m_ref[...] = jnp.full_like(m_ref, -jnp.inf)
    acc_ref[...] = jnp.zeros_like(acc_ref)

    def scores(start, size, s_ref):
        k = k_ref[0, pl.ds(start, size), :]
        for mp in range(2):
            s_ref[mp, :, 0:size] = _dot_nt(qs[mp], k)

    def consume(start, size, s_ref):
        v_ext = jnp.concatenate([v_ref[0, pl.ds(start, size), :], jnp.ones((size, LANES), BF16)], axis=1)
        for mp in range(2):
            s = s_ref[mp, :, 0:size]
            m_old = m_ref[mp]
            m_new = jnp.maximum(m_old, jnp.max(s, axis=-1, keepdims=True))
            p = jnp.exp(s - m_new).astype(BF16)
            acc_ref[mp] = jnp.exp(m_old - m_new) * acc_ref[mp] + _dot(p, v_ext)
            m_ref[mp] = m_new

    def latent(j):
        return pl.multiple_of(n_ctx + j * tk, math.gcd(n_ctx, tk))

    scores(0, n_ctx, sa_ref)
    if n_chunks == 0:
        consume(0, n_ctx, sa_ref)
    else:
        scores(latent(0), tk, sb_ref)
        consume(0, n_ctx, sa_ref)

        def chunk_pair(jj, carry):
            scores(latent(2 * jj + 1), tk, sa_ref)
            consume(latent(2 * jj), tk, sb_ref)
            scores(latent(jnp.minimum(2 * jj + 2, n_chunks - 1)), tk, sb_ref)
            consume(latent(2 * jj + 1), tk, sa_ref)
            return carry

        lax.fori_loop(0, n_chunks // 2, chunk_pair, 0)
    o_ref[0] = (acc_ref[0, :, 0:LANES] / acc_ref[0, :, LANES:]
                - lam * (acc_ref[1, :, 0:LANES] / acc_ref[1, :, LANES:]))


def _attention(qd, kd, vd, lam_vecs, lam_init, n_q, n_ctx, n_kv, tq, tk):
    bsz = qd.shape[0]
    n_chunks = (n_kv - n_ctx) // tk
    assert n_chunks % 2 == 0 and n_ctx + n_chunks * tk == n_kv and n_q % tq == 0
    lam_spec = pl.BlockSpec((1, DIFF_DH), lambda b, h, i: (0, 0))
    return pl.pallas_call(
        functools.partial(_attn_kernel, n_ctx=n_ctx, n_chunks=n_chunks, tk=tk, lam_init=lam_init),
        grid=(bsz, DIFF_HEADS, n_q // tq),
        in_specs=[lam_spec] * 4 + [
            pl.BlockSpec((1, tq, LANES), lambda b, h, i: (b, i, h)),
            pl.BlockSpec((1, n_kv, LANES), lambda b, h, i: (b, 0, h)),
            pl.BlockSpec((1, n_kv, LANES), lambda b, h, i: (b, 0, h))],
        out_specs=pl.BlockSpec((1, tq, LANES), lambda b, h, i: (b, i, h)),
        out_shape=jax.ShapeDtypeStruct((bsz, n_q, DIFF_V), F32),
        scratch_shapes=[pltpu.VMEM((2, tq, 1), F32), pltpu.VMEM((2, tq, 2 * LANES), F32),
                        pltpu.VMEM((2, tq, max(tk, n_ctx)), F32), pltpu.VMEM((2, tq, max(tk, n_ctx)), F32)],
        name="diff_attn",
        compiler_params=_cparams(("parallel", "parallel", "arbitrary")),
    )(*lam_vecs, qd, kd, vd)


def _postmix_kernel(x_ref, of_ref, ob_ref, og_ref, odc_ref, odl_ref, pc_ref, pp_ref, pn_ref,
                    gn_ref, dn_ref, pw_ref, ps_ref, wo_ref, g1_ref, sh2_ref, sc2_ref, n2_ref,
                    wr_ref, br_ref, x1_ref, h2_ref, route_ref, ext_ref,
                    *, tm, nct, n_ctx, seq, lam_init):
    i = pl.program_id(1)

    a = of_ref[0] + ob_ref[0]
    avg = (lax.broadcasted_iota(jnp.int32, (GLA_V, GLA_V), 0) // GLA_DV
           == lax.broadcasted_iota(jnp.int32, (GLA_V, GLA_V), 1) // GLA_DV).astype(F32) * (1.0 / GLA_DV)
    gla = a * lax.rsqrt(_dot_hi(a * a, avg) + EPS) * gn_ref[...] * _silu(og_ref[0])
    y = _dot(gla.astype(BF16), wo_ref[0:GLA_V, :])

    for hd in range(DIFF_HEADS):
        lo = hd * DIFF_DV
        od = jnp.where(i < nct, odc_ref[0, :, lo:lo + DIFF_DV], odl_ref[0, :, lo:lo + DIFF_DV])
        dh = _rms(od) * dn_ref[:, lo:lo + DIFF_DV] * (1.0 - lam_init)
        y = y + _dot(dh.astype(BF16), wo_ref[GLA_V + lo:GLA_V + lo + DIFF_DV, :])

    seg_lo = jnp.where(i < nct, 0, n_ctx)
    seg_hi = jnp.where(i < nct, n_ctx, seq)
    ext_ref[0:POOL_HALO] = pp_ref[0]
    ext_ref[POOL_HALO:POOL_HALO + tm] = pc_ref[0]
    ext_ref[POOL_HALO + tm:] = pn_ref[0]
    pos_e = i * tm - POOL_HALO + lax.broadcasted_iota(jnp.int32, (tm + 2 * POOL_HALO, 1), 0)
    e = jnp.where((pos_e >= seg_lo) & (pos_e < seg_hi), ext_ref[...], 0.0)
    pos = i * tm + lax.broadcasted_iota(jnp.int32, (tm, 1), 0)
    grp = lax.broadcasted_iota(jnp.int32, (1, POOL_W), 1) // POOL_CH
    run, width, mean = e, 1, jnp.zeros((tm, POOL_W), F32)
    for gi, w in enumerate(POOL_WINDOWS):
        while width < w:
            n = run.shape[0] - width
            run = run[0:n] + run[width:width + n]
            width *= 2
        start = POOL_HALO - w // 2
        cnt = (jnp.minimum(pos + (w - w // 2), seg_hi) - jnp.maximum(pos - w // 2, seg_lo)).astype(F32)
        mean = jnp.where(grp == gi, run[start:start + tm] / cnt, mean)
    pooled = _dot((mean - pc_ref[0]).astype(BF16), pw_ref[...]) * ps_ref[...]
    y = y + _dot(pooled.astype(BF16), wo_ref[GLA_V + DIFF_V:, :])

    x1 = x_ref[0] + g1_ref[0] * y
    x1_ref[0] = x1
    h2 = _rms(x1) * n2_ref[...] * (1.0 + sc2_ref[0]) + sh2_ref[0]
    bits = lax.bitcast_convert_type(h2.astype(BF16).astype(F32), jnp.uint32)
    half = bits.shape[1] // 2
    h2_ref[0] = (bits[:, 0:half] >> 16) | (bits[:, half:] & jnp.uint32(0xFFFF0000))

    logit = _dot_hi(h2, wr_ref[...]) + br_ref[...]
    lane = lax.broadcasted_iota(jnp.int32, (1, ROUTE_W), 1).astype(F32)
    neg = -jnp.inf

    def top(vals):
        mx = jnp.max(vals, axis=-1, keepdims=True)
        idx = jnp.min(jnp.where(vals == mx, lane, float(ROUTE_W)), axis=-1, keepdims=True)
        return mx, idx

    gl = jnp.where(lane < N_GROUPS, logit, neg)
    gmax, gidx = top(gl)
    g_top = 1.0 / jnp.sum(jnp.exp(gl - gmax), axis=-1, keepdims=True)
    e_lo = N_GROUPS + gidx * EXPERTS_PER_GROUP
    el = jnp.where((lane >= e_lo) & (lane < e_lo + EXPERTS_PER_GROUP), logit, neg)
    emax, idx1 = top(el)
    esum = jnp.sum(jnp.exp(el - emax), axis=-1, keepdims=True)
    emax2, idx2 = top(jnp.where(lane == idx1, neg, el))
    e1 = 1.0 / esum
    e2 = jnp.exp(emax2 - emax) / esum
    w1 = g_top * e1 / (e1 + e2)
    w2 = g_top * e2 / (e1 + e2)
    rec = jnp.where(lane == 0, idx1 - N_GROUPS, 0.0)
    rec = jnp.where(lane == 1, idx2 - N_GROUPS, rec)
    rec = jnp.where(lane == 2, w1, rec)
    route_ref[0] = jnp.where(lane == 3, w2, rec)


def _postmix(xall, o_f, o_b, og, od_ctx, od_lat, pool, mod, prm, nct, n_ctx, tm, lam_init):
    bsz, seq, d = xall.shape
    hpb = tm // POOL_HALO
    n_halo = seq // POOL_HALO

    def mod_spec(col):
        return pl.BlockSpec((1, 1, d), lambda b, i: (jnp.where(i < nct, bsz, b), 0, col))

    tile = lambda w: pl.BlockSpec((1, tm, w), lambda b, i: (b, i, 0))
    full = lambda r, c: pl.BlockSpec((r, c), lambda b, i: (0, 0))
    return pl.pallas_call(
        functools.partial(_postmix_kernel, tm=tm, nct=nct, n_ctx=n_ctx, seq=seq, lam_init=lam_init),
        grid=(bsz, seq // tm),
        in_specs=[tile(d), tile(GLA_V), tile(GLA_V), tile(GLA_V),
                  pl.BlockSpec((1, tm, DIFF_V), lambda b, i: (b, jnp.minimum(i, nct - 1), 0)),
                  pl.BlockSpec((1, tm, DIFF_V), lambda b, i: (b, jnp.maximum(i - nct, 0), 0)),
                  tile(POOL_W),
                  pl.BlockSpec((1, POOL_HALO, POOL_W), lambda b, i: (b, jnp.maximum(i * hpb - 1, 0), 0)),
                  pl.BlockSpec((1, POOL_HALO, POOL_W), lambda b, i: (b, jnp.minimum((i + 1) * hpb, n_halo - 1), 0)),
                  full(1, GLA_V), full(1, DIFF_V), full(POOL_W, POOL_W), full(1, POOL_W), full(d, d),
                  mod_spec(2), mod_spec(3), mod_spec(4), full(1, d), full(d, ROUTE_W), full(1, ROUTE_W)],
        out_specs=[tile(d), tile(d // 2), tile(ROUTE_W)],
        out_shape=[jax.ShapeDtypeStruct((bsz, seq, d), F32), jax.ShapeDtypeStruct((bsz, seq, d // 2), jnp.uint32),
                   jax.ShapeDtypeStruct((bsz, seq, ROUTE_W), F32)],
        scratch_shapes=[pltpu.VMEM((tm + 2 * POOL_HALO, POOL_W), F32)],
        name="postmix",
        compiler_params=_cparams(("parallel", "parallel")),
    )(xall, o_f, o_b, og, od_ctx, od_lat, pool, pool, pool, prm["gla_norm"], prm["diff_norm"], prm["pool_w"],
      prm["pool_scale"], prm["w_out"], mod, mod, mod, prm["norm2"], prm["w_route"], prm["b_route"])


def _slot_of_assignment(expert, n_tok):
    n_assign = n_tok * TOP_K
    e = expert.reshape(n_assign)
    hot = (e[:, None] == jnp.arange(N_EXPERTS, dtype=jnp.int32)[None, :]).astype(jnp.int32)
    csum = jnp.cumsum(hot, axis=0)
    counts = csum[-1]
    rank = jnp.sum(csum * hot, axis=1) - 1
    padded = (counts + MOE_BLOCK - 1) // MOE_BLOCK * MOE_BLOCK
    padded_end = jnp.cumsum(padded)
    dest = (padded_end - padded)[e] + rank
    n_slots = -(-n_assign // MOE_BLOCK) * MOE_BLOCK + N_EXPERTS * MOE_BLOCK
    block_start = jnp.arange(n_slots // MOE_BLOCK, dtype=jnp.int32) * MOE_BLOCK
    block_expert = jnp.minimum(jnp.sum((padded_end[None, :] <= block_start[:, None]).astype(jnp.int32), axis=1),
                               N_EXPERTS - 1)
    return dest, block_expert, n_slots


def _each(n, fn):
    def step(r, carry):
        fn(r)
        return carry
    lax.fori_loop(0, n, step, 0, unroll=8)


def _dispatch_kernel(dcur_ref, dprev_ref, h_hbm, xs_in, xs_hbm, sem, *, tile):
    del xs_in
    i = pl.program_id(0)
    n_tiles = pl.num_programs(0) - 1

    def copy(step, dest_ref, r, k):
        return pltpu.make_async_copy(h_hbm.at[pl.ds(step * tile + r, 1), :],
                                     xs_hbm.at[pl.ds(dest_ref[0, 0, r * TOP_K + k], 1), :], sem.at[step % 2])

    @pl.when(i < n_tiles)
    def _():
        _each(tile, lambda r: [copy(i, dcur_ref, r, k).start() for k in range(TOP_K)])

    @pl.when(i > 0)
    def _():
        _each(tile, lambda r: [copy(i - 1, dprev_ref, r, k).wait() for k in range(TOP_K)])


def _dispatch(h2p, dest, n_slots, tile):
    n_tok, w = h2p.shape
    n_tiles = n_tok // tile
    dest3 = dest.reshape(n_tiles, 1, tile * TOP_K)
    smem = lambda imap: pl.BlockSpec((1, 1, tile * TOP_K), imap, memory_space=pltpu.SMEM)
    return pl.pallas_call(
        functools.partial(_dispatch_kernel, tile=tile),
        grid=(n_tiles + 1,),
        in_specs=[smem(lambda i: (jnp.minimum(i, n_tiles - 1), 0, 0)), smem(lambda i: (jnp.maximum(i - 1, 0), 0, 0)),
                  pl.BlockSpec(memory_space=pl.ANY), pl.BlockSpec(memory_space=pl.ANY)],
        out_specs=pl.BlockSpec(memory_space=pl.ANY),
        out_shape=jax.ShapeDtypeStruct((n_slots, w), jnp.uint32),
        scratch_shapes=[pltpu.SemaphoreType.DMA((2,))],
        input_output_aliases={3: 0},
        name="moe_dispatch",
        compiler_params=_cparams(("arbitrary",)),
    )(dest3, dest3, h2p, jnp.zeros((n_slots, w), jnp.uint32))


def _expert_kernel(be_ref, xs_ref, w1_ref, w3_ref, w2_ref, ys_ref):
    del be_ref
    bits = xs_ref[...]
    half = bits.shape[1]
    x_lo = lax.bitcast_convert_type(bits << 16, F32).astype(BF16)
    x_hi = lax.bitcast_convert_type(bits & jnp.uint32(0xFFFF0000), F32).astype(BF16)

    def up(w_ref):
        return _dot(x_lo, w_ref[0, 0:half, :]) + _dot(x_hi, w_ref[0, half:, :])

    ys_ref[...] = _dot((_silu(up(w1_ref)) * up(w3_ref)).astype(BF16), w2_ref[0])


def _experts(xs, block_expert, w1, w3, w2):
    n_slots, half = xs.shape
    _, d, d_exp = w1.shape
    return pl.pallas_call(
        _expert_kernel,
        grid_spec=pltpu.PrefetchScalarGridSpec(
            num_scalar_prefetch=1,
            grid=(n_slots // MOE_BLOCK,),
            in_specs=[pl.BlockSpec((MOE_BLOCK, half), lambda i, be: (i, 0)),
                      pl.BlockSpec((1, d, d_exp), lambda i, be: (be[i], 0, 0)),
                      pl.BlockSpec((1, d, d_exp), lambda i, be: (be[i], 0, 0)),
                      pl.BlockSpec((1, d_exp, d), lambda i, be: (be[i], 0, 0))],
            out_specs=pl.BlockSpec((MOE_BLOCK, d), lambda i, be: (i, 0))),
        out_shape=jax.ShapeDtypeStruct((n_slots, d), F32),
        name="moe_experts",
        compiler_params=_cparams(("arbitrary",)),
    )(block_expert, xs, w1, w3, w2)


def _combine_kernel(dcur_ref, dnext_ref, x1_ref, route_ref, g2_ref, fn_ref, ys_hbm, o_ref, ybuf, sem,
                    *, final, tile):
    i = pl.program_id(0)
    n = pl.num_programs(0)

    def fetch(step, dest_ref, r, k):
        slot = step % 2
        return pltpu.make_async_copy(ys_hbm.at[pl.ds(dest_ref[0, 0, r * TOP_K + k], 1), :],
                                     ybuf.at[slot, k, pl.ds(r, 1), :], sem.at[slot])

    @pl.when(i == 0)
    def _():
        _each(tile, lambda r: [fetch(i, dcur_ref, r, k).start() for k in range(TOP_K)])

    @pl.when(i + 1 < n)
    def _():
        _each(tile, lambda r: [fetch(i + 1, dnext_ref, r, k).start() for k in range(TOP_K)])

    _each(tile, lambda r: [fetch(i, dcur_ref, r, k).wait() for k in range(TOP_K)])
    route = route_ref[0]
    lane = lax.broadcasted_iota(jnp.int32, (1, ROUTE_W), 1)
    w0 = jnp.sum(jnp.where(lane == 2, route, 0.0), axis=-1, keepdims=True)
    w1 = jnp.sum(jnp.where(lane == 3, route, 0.0), axis=-1, keepdims=True)
    x = x1_ref[0] + g2_ref[0] * (ybuf[i % 2, 0] * w0 + ybuf[i % 2, 1] * w1)
    o_ref[0] = _rms(x) * fn_ref[...] if final else x


def _combine(x1, ys, dest, route, mod, final_norm, nct, tm, final):
    bsz, seq, d = x1.shape
    off = nct if final else 0
    tpb = seq // tm - off
    per_b = seq // tm

    def tok_tile(i):
        return (i // tpb) * per_b + off + i % tpb

    n_steps = bsz * tpb
    dest3 = dest.reshape(bsz * per_b, 1, tm * TOP_K)
    smem = lambda imap: pl.BlockSpec((1, 1, tm * TOP_K), imap, memory_space=pltpu.SMEM)
    tile = lambda w: pl.BlockSpec((1, tm, w), lambda i: (i // tpb, off + i % tpb, 0))
    return pl.pallas_call(
        functools.partial(_combine_kernel, final=final, tile=tm),
        grid=(n_steps,),
        in_specs=[smem(lambda i: (tok_tile(i), 0, 0)),
                  smem(lambda i: (tok_tile(jnp.minimum(i + 1, n_steps - 1)), 0, 0)),
                  tile(d), tile(ROUTE_W),
                  pl.BlockSpec((1, 1, d), lambda i: (jnp.where(off + i % tpb < nct, bsz, i // tpb), 0, 5)),
                  pl.BlockSpec((1, d), lambda i: (0, 0)),
                  pl.BlockSpec(memory_space=pl.ANY)],
        out_specs=pl.BlockSpec((1, tm, d), lambda i: (i // tpb, i % tpb, 0)),
        out_shape=jax.ShapeDtypeStruct((bsz, tpb * tm, d), F32),
        scratch_shapes=[pltpu.VMEM((2, TOP_K, tm, d), F32), pltpu.SemaphoreType.DMA((2,))],
        name="combine",
        compiler_params=_cparams(("arbitrary",)),
    )(dest3, dest3, x1, route, mod, final_norm, ys)


def _rope_tables(n_ctx, n_lat):
    t = jnp.arange(n_lat, dtype=jnp.int32)
    row = (t // GRID_W).astype(F32)
    col = (t % GRID_W).astype(F32)
    inv = 1.0 / (ROPE_BASE ** (jnp.arange(0, AX_DIM, 2, dtype=F32) / AX_DIM))
    lane = jnp.arange(LANES)
    within = lane % DIFF_DH
    pos = jnp.where((within < AX_DIM)[None, :], row[:, None], col[:, None])
    ang = pos * inv[within % (AX_DIM // 2)][None, :]
    sign = jnp.where((within % AX_DIM) < AX_DIM // 2, -1.0, 1.0)[None, :]
    cos = jnp.concatenate([jnp.ones((n_ctx, LANES), F32), jnp.cos(ang)], axis=0)
    sin = jnp.concatenate([jnp.zeros((n_ctx, LANES), F32), jnp.sin(ang) * sign], axis=0)
    return cos, sin


def _pack_layer(layer, w_in, w_out, wa2_f, ba_f, wa2_b, ba_b, pool_w, wg, bg, we, be):
    d = w_in.shape[1]
    wi = w_in[layer]
    o = 0
    parts = {}
    for name, size in (("qg", GLA_K), ("kg", GLA_K), ("vg", GLA_V), ("og", GLA_V), ("af", GATE_RANK),
                       ("ab", GATE_RANK), ("qd", DIFF_QK), ("kd", DIFF_QK), ("vd", DIFF_V), ("pl", POOL_W)):
        parts[name] = wi[:, o:o + size]
        o += size
    gate = jnp.concatenate([parts["af"], parts["ab"], jnp.zeros((d, LANES - 2 * GATE_RANK), F32)], axis=1)
    w_all = jnp.concatenate([parts[n] for n in ("qg", "kg", "vg", "og", "qd", "kd", "vd", "pl")] + [gate],
                            axis=1).astype(BF16)
    wa2 = jnp.zeros((LANES, 2 * GLA_K), F32)
    wa2 = wa2.at[0:GATE_RANK, 0:GLA_K].set(wa2_f[layer])
    wa2 = wa2.at[GATE_RANK:2 * GATE_RANK, GLA_K:].set(wa2_b[layer])
    ba = jnp.concatenate([ba_f[layer], ba_b[layer]])[None, :]
    pw = jnp.zeros((POOL_W, POOL_W), F32)
    for gi in range(len(POOL_WINDOWS)):
        pw = pw.at[gi * POOL_CH:(gi + 1) * POOL_CH, gi * POOL_CH:(gi + 1) * POOL_CH].set(pool_w[layer, gi])
    w_route = jnp.concatenate([wg[layer], we[layer], jnp.zeros((d, ROUTE_W - N_GROUPS - N_EXPERTS), F32)], axis=1)
    b_route = jnp.concatenate([bg[layer], be[layer], jnp.zeros((ROUTE_W - N_GROUPS - N_EXPERTS,), F32)])[None, :]
    return dict(w_all=w_all, wa2=wa2, ba=ba, pool_w=pw.astype(BF16), w_out=w_out[layer].astype(BF16),
                w_route=w_route, b_route=b_route)


def kernel(x, c, ctx, c_ctx, w_mod, b_mod, norm1, norm2, w_in, w_out, gla_wa2_f, gla_ba_f, gla_wa2_b, gla_ba_b, gla_norm, lam_q1, lam_k1, lam_q2, lam_k2, diff_norm, pool_w, pool_scale, router_wg, router_bg, router_we, router_be, exp_w1, exp_w3, exp_w2, final_norm):
    bsz, n_lat, d = x.shape
    n_ctx = ctx.shape[1]
    depth = w_mod.shape[0]
    seq = n_ctx + n_lat
    tm = math.gcd(256, n_ctx)
    nct = n_ctx // tm
    tq = math.gcd(512, n_lat)
    tk = math.gcd(1024, n_lat // 2)
    gla_rows = math.gcd(64, n_ctx)
    assert bsz + 1 <= 8 and n_lat % tm == 0 and n_lat % GRID_W == 0

    cond = jnp.concatenate([c, c_ctx[None, :], jnp.zeros((8 - bsz - 1, d), F32)], axis=0)
    mod_all = _adaln(cond, w_mod, b_mod)
    cos_t, sin_t = _rope_tables(n_ctx, n_lat)
    xall = jnp.concatenate([ctx, x], axis=1)
    n_tok = bsz * seq

    for layer in range(depth):
        last = layer == depth - 1
        lam_init = 0.8 - 0.6 * math.exp(-0.3 * layer)
        prm = _pack_layer(layer, w_in, w_out, gla_wa2_f, gla_ba_f, gla_wa2_b, gla_ba_b, pool_w,
                          router_wg, router_bg, router_we, router_be)
        prm.update(gla_norm=gla_norm[layer][None, :], diff_norm=diff_norm[layer][None, :],
                   pool_scale=pool_scale[layer][None, :], norm2=norm2[layer][None, :])
        mod = mod_all[layer].reshape(8, 1, 6 * d)

        qg, kg, vg, og, la, qd, kd, vd, pool = _premix(
            xall, mod, norm1[layer][None, :], prm["w_all"], prm["wa2"], prm["ba"], cos_t, sin_t, nct, tm)
        o_f, o_b = _gla(qg, kg, vg, la, n_ctx, gla_rows)
        lam_vecs = [v[layer][None, :] for v in (lam_q1, lam_k1, lam_q2, lam_k2)]
        od_lat = _attention(qd[:, n_ctx:], kd, vd, lam_vecs, lam_init, n_lat, n_ctx, seq, tq, tk)
        od_ctx = _attention(qd, kd, vd, lam_vecs, lam_init, n_ctx, n_ctx, n_ctx, tm, tk)
        x1, h2p, route = _postmix(xall, o_f, o_b, og, od_ctx, od_lat, pool, mod, prm, nct, n_ctx, tm, lam_init)

        expert = route[..., 0:TOP_K].astype(jnp.int32).reshape(n_tok, TOP_K)
        dest, block_expert, n_slots = _slot_of_assignment(expert, n_tok)
        xs = _dispatch(h2p.reshape(n_tok, d // 2), dest, n_slots, tm)
        ys = _experts(xs, block_expert, exp_w1[layer].astype(BF16), exp_w3[layer].astype(BF16),
                      exp_w2[layer].astype(BF16))
        xall = _combine(x1, ys, dest, route, mod, final_norm[None, :], nct, tm, last)
    return xall
```

```python
import functools
import math

import jax
import jax.numpy as jnp
from jax import lax
from jax.experimental import pallas as pl
from jax.experimental.pallas import tpu as pltpu

F32 = jnp.float32
BF16 = jnp.bfloat16
HIGHEST = lax.Precision.HIGHEST

EPS = 1e-6
GRID_W = 64
GLA_HEADS, GLA_DK, GLA_DV = 4, 32, 64
GLA_K, GLA_V = GLA_HEADS * GLA_DK, GLA_HEADS * GLA_DV
GATE_RANK, GATE_TEMP = 16, 16.0
DIFF_HEADS, DIFF_DH = 4, 64
DIFF_DV = 2 * DIFF_DH
DIFF_QK = DIFF_HEADS * 2 * DIFF_DH
DIFF_V = DIFF_HEADS * DIFF_DV
ROPE_BASE = 10000.0
AX_DIM = DIFF_DH // 2
POOL_WINDOWS = (2, 4, 8, 16)
POOL_CH = 64
POOL_W = len(POOL_WINDOWS) * POOL_CH
POOL_HALO = 8
N_GROUPS, EXPERTS_PER_GROUP = 4, 4
N_EXPERTS = N_GROUPS * EXPERTS_PER_GROUP
TOP_K = 2
MOE_BLOCK = 256

LANES = 128
SUB = 16
ROUTE_W = LANES
VMEM_LIMIT = 56 * 1024 * 1024


def _cparams(sem):
    return pltpu.CompilerParams(dimension_semantics=sem, vmem_limit_bytes=VMEM_LIMIT)


def _dot(a, b):
    return jnp.dot(a, b, preferred_element_type=F32)


def _dot_hi(a, b):
    return jnp.dot(a, b, precision=HIGHEST, preferred_element_type=F32)


def _dot_nt(a, b):
    return lax.dot_general(a, b, (((1,), (1,)), ((), ())), preferred_element_type=F32)


def _dot_tn(a, b):
    return lax.dot_general(a, b, (((0,), (0,)), ((), ())), preferred_element_type=F32)


def _silu(x):
    return x * jax.nn.sigmoid(x)


def _log_sigmoid(x):
    return jnp.minimum(x, 0.0) - jnp.log1p(jnp.exp(-jnp.abs(x)))


def _rms(x):
    return x * lax.rsqrt(jnp.mean(x * x, axis=-1, keepdims=True) + EPS)


def _adaln_kernel(c_ref, w_ref, b_ref, o_ref):
    o_ref[0] = _dot_hi(_silu(c_ref[...]), w_ref[0]) + b_ref[0]


def _adaln(cond, w_mod, b_mod):
    depth, d, six_d = w_mod.shape
    tn = 1536
    return pl.pallas_call(
        _adaln_kernel,
        grid=(depth, six_d // tn),
        in_specs=[pl.BlockSpec((8, d), lambda l, j: (0, 0)),
                  pl.BlockSpec((1, d, tn), lambda l, j: (l, 0, j)),
                  pl.BlockSpec((1, 1, tn), lambda l, j: (l, 0, j))],
        out_specs=pl.BlockSpec((1, 8, tn), lambda l, j: (l, 0, j)),
        out_shape=jax.ShapeDtypeStruct((depth, 8, six_d), F32),
        name="adaln",
        compiler_params=_cparams(("arbitrary", "arbitrary")),
    )(cond, w_mod, b_mod.reshape(depth, 1, six_d))


_C_QG, _C_KG, _C_VG, _C_OG = 0, GLA_K, 2 * GLA_K, 2 * GLA_K + GLA_V
_C_QD = 2 * GLA_K + 2 * GLA_V
_C_KD = _C_QD + DIFF_QK
_C_VD = _C_KD + DIFF_QK
_C_PL = _C_VD + DIFF_V
_C_GT = _C_PL + POOL_W
_C_END = _C_GT + LANES


def _premix_kernel(x_ref, sh_ref, sc_ref, n1_ref, w_ref, wa2_ref, ba_ref, cos_ref, sin_ref,
                   qg_ref, kg_ref, vg_ref, og_ref, la_ref, qd_ref, kd_ref, vd_ref, pool_ref):
    x = x_ref[0]
    h = _rms(x) * n1_ref[...]
    hb = (h * (1.0 + sc_ref[0]) + sh_ref[0]).astype(BF16)

    def proj(lo, hi):
        return _dot(hb, w_ref[:, lo:hi])

    qg_ref[0] = proj(_C_QG, _C_KG) * (GLA_DK ** -0.5)
    kg_ref[0] = proj(_C_KG, _C_VG)
    vg_ref[0] = proj(_C_VG, _C_OG).astype(BF16)
    og_ref[0] = proj(_C_OG, _C_QD)
    vd_ref[0] = proj(_C_VD, _C_PL).astype(BF16)
    pool_ref[0] = proj(_C_PL, _C_GT)
    pre = _dot_hi(proj(_C_GT, _C_END), wa2_ref[...]) + ba_ref[...]
    la_ref[0] = _log_sigmoid(pre) / GATE_TEMP

    cos = cos_ref[...]
    sin = sin_ref[...]
    lane = lax.broadcasted_iota(jnp.int32, (1, LANES), 1)
    first_half = (lane % AX_DIM) < (AX_DIM // 2)

    def rope(a):
        partner = jnp.where(first_half, pltpu.roll(a, LANES - AX_DIM // 2, 1), pltpu.roll(a, AX_DIM // 2, 1))
        return a * cos + partner * sin

    for hd in range(DIFF_HEADS):
        lo = hd * LANES
        qd_ref[0, :, lo:lo + LANES] = (rope(proj(_C_QD + lo, _C_QD + lo + LANES)) * (DIFF_DH ** -0.5)).astype(BF16)
        kd_ref[0, :, lo:lo + LANES] = rope(proj(_C_KD + lo, _C_KD + lo + LANES)).astype(BF16)


def _premix(xall, mod, norm1, w_all, wa2, ba, cos_t, sin_t, nct, tm):
    bsz, seq, d = xall.shape

    def mod_row(b, i):
        return jnp.where(i < nct, bsz, b)

    tile = lambda w: pl.BlockSpec((1, tm, w), lambda b, i: (b, i, 0))
    outs = [(GLA_K, F32), (GLA_K, F32), (GLA_V, BF16), (GLA_V, F32), (2 * GLA_K, F32),
            (DIFF_QK, BF16), (DIFF_QK, BF16), (DIFF_V, BF16), (POOL_W, F32)]
    return pl.pallas_call(
        _premix_kernel,
        grid=(bsz, seq // tm),
        in_specs=[tile(d),
                  pl.BlockSpec((1, 1, d), lambda b, i: (mod_row(b, i), 0, 0)),
                  pl.BlockSpec((1, 1, d), lambda b, i: (mod_row(b, i), 0, 1)),
                  pl.BlockSpec((1, d), lambda b, i: (0, 0)),
                  pl.BlockSpec((d, _C_END), lambda b, i: (0, 0)),
                  pl.BlockSpec((LANES, 2 * GLA_K), lambda b, i: (0, 0)),
                  pl.BlockSpec((1, 2 * GLA_K), lambda b, i: (0, 0)),
                  pl.BlockSpec((tm, LANES), lambda b, i: (i, 0)),
                  pl.BlockSpec((tm, LANES), lambda b, i: (i, 0))],
        out_specs=[tile(w) for w, _ in outs],
        out_shape=[jax.ShapeDtypeStruct((bsz, seq, w), dt) for w, dt in outs],
        name="premix",
        compiler_params=_cparams(("parallel", "parallel")),
    )(xall, mod, mod, norm1, w_all, wa2, ba, cos_t, sin_t)


def _gla_kernel(qf_ref, kf_ref, vf_ref, gf_ref, qb_ref, kb_ref, vb_ref, gb_ref,
                of_ref, ob_ref, st_ref, *, bsz, rows):
    @pl.when(pl.program_id(0) == 0)
    def _():
        st_ref[...] = jnp.zeros_like(st_ref)

    n_sub = rows // SUB
    rr = lax.broadcasted_iota(jnp.int32, (rows, rows), 0)
    cc = lax.broadcasted_iota(jnp.int32, (rows, rows), 1)
    same_sub = rr // SUB == cc // SUB
    tri = ((same_sub & (cc <= rr)).astype(F32), (same_sub & (cc >= rr)).astype(F32))
    t_idx = lax.broadcasted_iota(jnp.int32, (SUB, 1), 0)
    same_head = (lax.broadcasted_iota(jnp.int32, (GLA_V, GLA_K), 0) // GLA_DV
                 == lax.broadcasted_iota(jnp.int32, (GLA_V, GLA_K), 1) // GLA_DK)
    expand = (lax.broadcasted_iota(jnp.int32, (GLA_K, GLA_V), 0) // GLA_DK
              == lax.broadcasted_iota(jnp.int32, (GLA_K, GLA_V), 1) // GLA_DV).astype(BF16)
    chains = [(b, 0, qf_ref, kf_ref, vf_ref, gf_ref, of_ref) for b in range(bsz)]
    chains += [(b, 1, qb_ref, kb_ref, vb_ref, gb_ref, ob_ref) for b in range(bsz)]

    for b, rev, q_ref, k_ref, v_ref, g_ref, o_ref in chains:
        bloc_all = _dot_hi(tri[rev], g_ref[b])
        order = range(n_sub - 1, -1, -1) if rev else range(n_sub)
        parts = {}
        for j in order:
            rs = slice(j * SUB, (j + 1) * SUB)
            q = q_ref[b, rs, :]
            k = k_ref[b, rs, :]
            vb = v_ref[b, rs, :]
            v = vb.astype(F32)
            bloc = bloc_all[rs]
            bend = bloc[0:1] if rev else bloc[SUB - 1:SUB]
            qk_terms = []
            for s in range(SUB):
                valid = (t_idx <= s) if rev else (t_idx >= s)
                rel = jnp.where(valid, bloc - bloc[s:s + 1], -jnp.inf)
                qk_terms.append((q * k[s:s + 1] * jnp.exp(rel)).astype(BF16))
            att = _dot(jnp.concatenate(qk_terms, axis=0), expand)
            o_diag = att[0:SUB] * v[0:1]
            for s in range(1, SUB):
                o_diag = o_diag + att[s * SUB:(s + 1) * SUB] * v[s:s + 1]
            upd = _dot_tn(vb, (k * jnp.exp(bend - bloc)).astype(BF16))
            parts[j] = ((q * jnp.exp(bloc)).astype(BF16), o_diag, jnp.exp(bend), jnp.where(same_head, upd, 0.0))
        st = st_ref[2 * b + rev]
        for j in order:
            q_dec, o_diag, decay, upd = parts[j]
            o_ref[b, j * SUB:(j + 1) * SUB, :] = _dot_nt(q_dec, st.astype(BF16)) + o_diag
            st = st * decay + upd
        st_ref[2 * b + rev] = st


def _gla(qg, kg, vg, la, n_ctx, rows):
    bsz, seq, _ = qg.shape
    nc = n_ctx // rows
    nblk = seq // rows

    def fwd(i):
        return (0, i, 0)

    def bwd_blk(i):
        return jnp.where(i < nc, nc - 1 - i, nblk + nc - 1 - i)

    def bwd(i):
        return (0, bwd_blk(i), 0)

    def spec(w, imap):
        return pl.BlockSpec((bsz, rows, w), imap)

    return pl.pallas_call(
        functools.partial(_gla_kernel, bsz=bsz, rows=rows),
        grid=(nblk,),
        in_specs=[spec(GLA_K, fwd), spec(GLA_K, fwd), spec(GLA_V, fwd), spec(GLA_K, fwd),
                  spec(GLA_K, bwd), spec(GLA_K, bwd), spec(GLA_V, bwd),
                  spec(GLA_K, lambda i: (0, bwd_blk(i), 1))],
        out_specs=[spec(GLA_V, fwd), spec(GLA_V, bwd)],
        out_shape=[jax.ShapeDtypeStruct((bsz, seq, GLA_V), F32)] * 2,
        scratch_shapes=[pltpu.VMEM((2 * bsz, GLA_V, GLA_K), F32)],
        name="gla_scan",
        compiler_params=_cparams(("arbitrary",)),
    )(qg, kg, vg, la, qg, kg, vg, la)


def _attn_kernel(lq1_ref, lk1_ref, lq2_ref, lk2_ref, q_ref, k_ref, v_ref, o_ref,
                 m_ref, acc_ref, sa_ref, sb_ref, *, n_ctx, n_chunks, tk, lam_init):
    lam = (jnp.exp(jnp.sum(lq1_ref[...] * lk1_ref[...], axis=-1, keepdims=True))
           - jnp.exp(jnp.sum(lq2_ref[...] * lk2_ref[...], axis=-1, keepdims=True)) + lam_init)
    q = q_ref[0]
    lane = lax.broadcasted_iota(jnp.int32, (1, LANES), 1)
    zero = jnp.zeros_like(q)
    qs = (jnp.where(lane < DIFF_DH, q, zero), jnp.where(lane >= DIFF_DH, q, zero))
    m_ref[...] = jnp.full_like(m_ref, -jnp.inf)
    acc_ref[...] = jnp.zeros_like(acc_ref)

    def scores(start, size, s_ref):
        k = k_ref[0, pl.ds(start, size), :]
        for mp in range(2):
            s_ref[mp, :, 0:size] = _dot_nt(qs[mp], k)

    def consume(start, size, s_ref):
        v_ext = jnp.concatenate([v_ref[0, pl.ds(start, size), :], jnp.ones((size, LANES), BF16)], axis=1)
        for mp in range(2):
            s = s_ref[mp, :, 0:size]
            m_old = m_ref[mp]
            m_new = jnp.maximum(m_old, jnp.max(s, axis=-1, keepdims=True))
            p = jnp.exp(s - m_new).astype(BF16)
            acc_ref[mp] = jnp.exp(m_old - m_new) * acc_ref[mp] + _dot(p, v_ext)
            m_ref[mp] = m_new

    def latent(j):
        return pl.multiple_of(n_ctx + j * tk, math.gcd(n_ctx, tk))

    scores(0, n_ctx, sa_ref)
    if n_chunks == 0:
        consume(0, n_ctx, sa_ref)
    else:
        scores(latent(0), tk, sb_ref)
        consume(0, n_ctx, sa_ref)

        def chunk_pair(jj, carry):
            scores(latent(2 * jj + 1), tk, sa_ref)
            consume(latent(2 * jj), tk, sb_ref)
            scores(latent(jnp.minimum(2 * jj + 2, n_chunks - 1)), tk, sb_ref)
            consume(latent(2 * jj + 1), tk, sa_ref)
            return carry

        lax.fori_loop(0, n_chunks // 2, chunk_pair, 0)
    o_ref[0] = (acc_ref[0, :, 0:LANES] / acc_ref[0, :, LANES:]
                - lam * (acc_ref[1, :, 0:LANES] / acc_ref[1, :, LANES:]))


def _attention(qd, kd, vd, lam_vecs, lam_init, n_q, n_ctx, n_kv, tq, tk):
    bsz = qd.shape[0]
    n_chunks = (n_kv - n_ctx) // tk
    assert n_chunks % 2 == 0 and n_ctx + n_chunks * tk == n_kv and n_q % tq == 0
    lam_spec = pl.BlockSpec((1, DIFF_DH), lambda b, h, i: (0, 0))
    return pl.pallas_call(
        functools.partial(_attn_kernel, n_ctx=n_ctx, n_chunks=n_chunks, tk=tk, lam_init=lam_init),
        grid=(bsz, DIFF_HEADS, n_q // tq),
        in_specs=[lam_spec] * 4 + [
            pl.BlockSpec((1, tq, LANES), lambda b, h, i: (b, i, h)),
            pl.BlockSpec((1, n_kv, LANES), lambda b, h, i: (b, 0, h)),
            pl.BlockSpec((1, n_kv, LANES), lambda b, h, i: (b, 0, h))],
        out_specs=pl.BlockSpec((1, tq, LANES), lambda b, h, i: (b, i, h)),
        out_shape=jax.ShapeDtypeStruct((bsz, n_q, DIFF_V), F32),
        scratch_shapes=[pltpu.VMEM((2, tq, 1), F32), pltpu.VMEM((2, tq, 2 * LANES), F32),
                        pltpu.VMEM((2, tq, max(tk, n_ctx)), F32), pltpu.VMEM((2, tq, max(tk, n_ctx)), F32)],
        name="diff_attn",
        compiler_params=_cparams(("parallel", "parallel", "arbitrary")),
    )(*lam_vecs, qd, kd, vd)


def _postmix_kernel(x_ref, of_ref, ob_ref, og_ref, odc_ref, odl_ref, pc_ref, pp_ref, pn_ref,
                    gn_ref, dn_ref, pw_ref, ps_ref, wo_ref, g1_ref, sh2_ref, sc2_ref, n2_ref,
                    wr_ref, br_ref, x1_ref, h2_ref, route_ref, ext_ref,
                    *, tm, nct, n_ctx, seq, lam_init):
    i = pl.program_id(1)

    a = of_ref[0] + ob_ref[0]
    avg = (lax.broadcasted_iota(jnp.int32, (GLA_V, GLA_V), 0) // GLA_DV
           == lax.broadcasted_iota(jnp.int32, (GLA_V, GLA_V), 1) // GLA_DV).astype(F32) * (1.0 / GLA_DV)
    gla = a * lax.rsqrt(_dot_hi(a * a, avg) + EPS) * gn_ref[...] * _silu(og_ref[0])
    y = _dot(gla.astype(BF16), wo_ref[0:GLA_V, :])

    for hd in range(DIFF_HEADS):
        lo = hd * DIFF_DV
        od = jnp.where(i < nct, odc_ref[0, :, lo:lo + DIFF_DV], odl_ref[0, :, lo:lo + DIFF_DV])
        dh = _rms(od) * dn_ref[:, lo:lo + DIFF_DV] * (1.0 - lam_init)
        y = y + _dot(dh.astype(BF16), wo_ref[GLA_V + lo:GLA_V + lo + DIFF_DV, :])

    seg_lo = jnp.where(i < nct, 0, n_ctx)
    seg_hi = jnp.where(i < nct, n_ctx, seq)
    ext_ref[0:POOL_HALO] = pp_ref[0]
    ext_ref[POOL_HALO:POOL_HALO + tm] = pc_ref[0]
    ext_ref[POOL_HALO + tm:] = pn_ref[0]
    pos_e = i * tm - POOL_HALO + lax.broadcasted_iota(jnp.int32, (tm + 2 * POOL_HALO, 1), 0)
    e = jnp.where((pos_e >= seg_lo) & (pos_e < seg_hi), ext_ref[...], 0.0)
    pos = i * tm + lax.broadcasted_iota(jnp.int32, (tm, 1), 0)
    grp = lax.broadcasted_iota(jnp.int32, (1, POOL_W), 1) // POOL_CH
    run, width, mean = e, 1, jnp.zeros((tm, POOL_W), F32)
    for gi, w in enumerate(POOL_WINDOWS):
        while width < w:
            n = run.shape[0] - width
            run = run[0:n] + run[width:width + n]
            width *= 2
        start = POOL_HALO - w // 2
        cnt = (jnp.minimum(pos + (w - w // 2), seg_hi) - jnp.maximum(pos - w // 2, seg_lo)).astype(F32)
        mean = jnp.where(grp == gi, run[start:start + tm] / cnt, mean)
    pooled = _dot((mean - pc_ref[0]).astype(BF16), pw_ref[...]) * ps_ref[...]
    y = y + _dot(pooled.astype(BF16), wo_ref[GLA_V + DIFF_V:, :])

    x1 = x_ref[0] + g1_ref[0] * y
    x1_ref[0] = x1
    h2 = _rms(x1) * n2_ref[...] * (1.0 + sc2_ref[0]) + sh2_ref[0]
    bits = lax.bitcast_convert_type(h2.astype(BF16).astype(F32), jnp.uint32)
    half = bits.shape[1] // 2
    h2_ref[0] = (bits[:, 0:half] >> 16) | (bits[:, half:] & jnp.uint32(0xFFFF0000))

    logit = _dot_hi(h2, wr_ref[...]) + br_ref[...]
    lane = lax.broadcasted_iota(jnp.int32, (1, ROUTE_W), 1).astype(F32)
    neg = -jnp.inf

    def top(vals):
        mx = jnp.max(vals, axis=-1, keepdims=True)
        idx = jnp.min(jnp.where(vals == mx, lane, float(ROUTE_W)), axis=-1, keepdims=True)
        return mx, idx

    gl = jnp.where(lane < N_GROUPS, logit, neg)
    gmax, gidx = top(gl)
    g_top = 1.0 / jnp.sum(jnp.exp(gl - gmax), axis=-1, keepdims=True)
    e_lo = N_GROUPS + gidx * EXPERTS_PER_GROUP
    el = jnp.where((lane >= e_lo) & (lane < e_lo + EXPERTS_PER_GROUP), logit, neg)
    emax, idx1 = top(el)
    esum = jnp.sum(jnp.exp(el - emax), axis=-1, keepdims=True)
    emax2, idx2 = top(jnp.where(lane == idx1, neg, el))
    e1 = 1.0 / esum
    e2 = jnp.exp(emax2 - emax) / esum
    w1 = g_top * e1 / (e1 + e2)
    w2 = g_top * e2 / (e1 + e2)
    rec = jnp.where(lane == 0, idx1 - N_GROUPS, 0.0)
    rec = jnp.where(lane == 1, idx2 - N_GROUPS, rec)
    rec = jnp.where(lane == 2, w1, rec)
    route_ref[0] = jnp.where(lane == 3, w2, rec)


def _postmix(xall, o_f, o_b, og, od_ctx, od_lat, pool, mod, prm, nct, n_ctx, tm, lam_init):
    bsz, seq, d = xall.shape
    hpb = tm // POOL_HALO
    n_halo = seq // POOL_HALO

    def mod_spec(col):
        return pl.BlockSpec((1, 1, d), lambda b, i: (jnp.where(i < nct, bsz, b), 0, col))

    tile = lambda w: pl.BlockSpec((1, tm, w), lambda b, i: (b, i, 0))
    full = lambda r, c: pl.BlockSpec((r, c), lambda b, i: (0, 0))
    return pl.pallas_call(
        functools.partial(_postmix_kernel, tm=tm, nct=nct, n_ctx=n_ctx, seq=seq, lam_init=lam_init),
        grid=(bsz, seq // tm),
        in_specs=[tile(d), tile(GLA_V), tile(GLA_V), tile(GLA_V),
                  pl.BlockSpec((1, tm, DIFF_V), lambda b, i: (b, jnp.minimum(i, nct - 1), 0)),
                  pl.BlockSpec((1, tm, DIFF_V), lambda b, i: (b, jnp.maximum(i - nct, 0), 0)),
                  tile(POOL_W),
                  pl.BlockSpec((1, POOL_HALO, POOL_W), lambda b, i: (b, jnp.maximum(i * hpb - 1, 0), 0)),
                  pl.BlockSpec((1, POOL_HALO, POOL_W), lambda b, i: (b, jnp.minimum((i + 1) * hpb, n_halo - 1), 0)),
                  full(1, GLA_V), full(1, DIFF_V), full(POOL_W, POOL_W), full(1, POOL_W), full(d, d),
                  mod_spec(2), mod_spec(3), mod_spec(4), full(1, d), full(d, ROUTE_W), full(1, ROUTE_W)],
        out_specs=[tile(d), tile(d // 2), tile(ROUTE_W)],
        out_shape=[jax.ShapeDtypeStruct((bsz, seq, d), F32), jax.ShapeDtypeStruct((bsz, seq, d // 2), jnp.uint32),
                   jax.ShapeDtypeStruct((bsz, seq, ROUTE_W), F32)],
        scratch_shapes=[pltpu.VMEM((tm + 2 * POOL_HALO, POOL_W), F32)],
        name="postmix",
        compiler_params=_cparams(("parallel", "parallel")),
    )(xall, o_f, o_b, og, od_ctx, od_lat, pool, pool, pool, prm["gla_norm"], prm["diff_norm"], prm["pool_w"],
      prm["pool_scale"], prm["w_out"], mod, mod, mod, prm["norm2"], prm["w_route"], prm["b_route"])


def _slot_of_assignment(expert, n_tok):
    n_assign = n_tok * TOP_K
    e = expert.reshape(n_assign)
    hot = (e[:, None] == jnp.arange(N_EXPERTS, dtype=jnp.int32)[None, :]).astype(jnp.int32)
    csum = jnp.cumsum(hot, axis=0)
    counts = csum[-1]
    rank = jnp.sum(csum * hot, axis=1) - 1
    padded = (counts + MOE_BLOCK - 1) // MOE_BLOCK * MOE_BLOCK
    padded_end = jnp.cumsum(padded)
    dest = (padded_end - padded)[e] + rank
    n_slots = -(-n_assign // MOE_BLOCK) * MOE_BLOCK + N_EXPERTS * MOE_BLOCK
    block_start = jnp.arange(n_slots // MOE_BLOCK, dtype=jnp.int32) * MOE_BLOCK
    block_expert = jnp.minimum(jnp.sum((padded_end[None, :] <= block_start[:, None]).astype(jnp.int32), axis=1),
                               N_EXPERTS - 1)
    return dest, block_expert, n_slots


def _each(n, fn):
    def step(r, carry):
        fn(r)
        return carry
    lax.fori_loop(0, n, step, 0, unroll=8)


def _dispatch_kernel(dcur_ref, dprev_ref, h_ref, xs_in, xs_hbm, buf, sem, *, tile):
    del xs_in
    i = pl.program_id(0)
    n_tiles = pl.num_programs(0) - 1

    def copy(step, dest_ref, r, k):
        return pltpu.make_async_copy(buf.at[step % 2, pl.ds(r, 1), :],
                                     xs_hbm.at[pl.ds(dest_ref[0, 0, r * TOP_K + k], 1), :], sem.at[step % 2])

    @pl.when(i < n_tiles)
    def _():
        buf[i % 2] = h_ref[...]
        _each(tile, lambda r: [copy(i, dcur_ref, r, k).start() for k in range(TOP_K)])

    @pl.when(i > 0)
    def _():
        _each(tile, lambda r: [copy(i - 1, dprev_ref, r, k).wait() for k in range(TOP_K)])


def _dispatch(h2p, dest, n_slots, tile):
    n_tok, w = h2p.shape
    n_tiles = n_tok // tile
    dest3 = dest.reshape(n_tiles, 1, tile * TOP_K)
    smem = lambda imap: pl.BlockSpec((1, 1, tile * TOP_K), imap, memory_space=pltpu.SMEM)
    return pl.pallas_call(
        functools.partial(_dispatch_kernel, tile=tile),
        grid=(n_tiles + 1,),
        in_specs=[smem(lambda i: (jnp.minimum(i, n_tiles - 1), 0, 0)), smem(lambda i: (jnp.maximum(i - 1, 0), 0, 0)),
                  pl.BlockSpec((tile, w), lambda i: (jnp.minimum(i, n_tiles - 1), 0)),
                  pl.BlockSpec(memory_space=pl.ANY)],
        out_specs=pl.BlockSpec(memory_space=pl.ANY),
        out_shape=jax.ShapeDtypeStruct((n_slots, w), jnp.uint32),
        scratch_shapes=[pltpu.VMEM((2, tile, w), jnp.uint32), pltpu.SemaphoreType.DMA((2,))],
        input_output_aliases={3: 0},
        name="moe_dispatch",
        compiler_params=_cparams(("arbitrary",)),
    )(dest3, dest3, h2p, jnp.zeros((n_slots, w), jnp.uint32))


def _expert_kernel(be_ref, xs_ref, w1_ref, w3_ref, w2_ref, ys_ref):
    del be_ref
    bits = xs_ref[...]
    half = bits.shape[1]
    x_lo = lax.bitcast_convert_type(bits << 16, F32).astype(BF16)
    x_hi = lax.bitcast_convert_type(bits & jnp.uint32(0xFFFF0000), F32).astype(BF16)

    def up(w_ref):
        return _dot(x_lo, w_ref[0, 0:half, :]) + _dot(x_hi, w_ref[0, half:, :])

    ys_ref[...] = _dot((_silu(up(w1_ref)) * up(w3_ref)).astype(BF16), w2_ref[0])


def _experts(xs, block_expert, w1, w3, w2):
    n_slots, half = xs.shape
    _, d, d_exp = w1.shape
    return pl.pallas_call(
        _expert_kernel,
        grid_spec=pltpu.PrefetchScalarGridSpec(
            num_scalar_prefetch=1,
            grid=(n_slots // MOE_BLOCK,),
            in_specs=[pl.BlockSpec((MOE_BLOCK, half), lambda i, be: (i, 0)),
                      pl.BlockSpec((1, d, d_exp), lambda i, be: (be[i], 0, 0)),
                      pl.BlockSpec((1, d, d_exp), lambda i, be: (be[i], 0, 0)),
                      pl.BlockSpec((1, d_exp, d), lambda i, be: (be[i], 0, 0))],
            out_specs=pl.BlockSpec((MOE_BLOCK, d), lambda i, be: (i, 0))),
        out_shape=jax.ShapeDtypeStruct((n_slots, d), F32),
        name="moe_experts",
        compiler_params=_cparams(("arbitrary",)),
    )(block_expert, xs, w1, w3, w2)


def _combine_kernel(dcur_ref, dnext_ref, x1_ref, route_ref, g2_ref, fn_ref, ys_hbm, o_ref, ybuf, sem,
                    *, final, tile):
    i = pl.program_id(0)
    n = pl.num_programs(0)

    def fetch(step, dest_ref, r, k):
        slot = step % 2
        return pltpu.make_async_copy(ys_hbm.at[pl.ds(dest_ref[0, 0, r * TOP_K + k], 1), :],
                                     ybuf.at[slot, k, pl.ds(r, 1), :], sem.at[slot])

    @pl.when(i == 0)
    def _():
        _each(tile, lambda r: [fetch(i, dcur_ref, r, k).start() for k in range(TOP_K)])

    @pl.when(i + 1 < n)
    def _():
        _each(tile, lambda r: [fetch(i + 1, dnext_ref, r, k).start() for k in range(TOP_K)])

    _each(tile, lambda r: [fetch(i, dcur_ref, r, k).wait() for k in range(TOP_K)])
    route = route_ref[0]
    lane = lax.broadcasted_iota(jnp.int32, (1, ROUTE_W), 1)
    w0 = jnp.sum(jnp.where(lane == 2, route, 0.0), axis=-1, keepdims=True)
    w1 = jnp.sum(jnp.where(lane == 3, route, 0.0), axis=-1, keepdims=True)
    x = x1_ref[0] + g2_ref[0] * (ybuf[i % 2, 0] * w0 + ybuf[i % 2, 1] * w1)
    o_ref[0] = _rms(x) * fn_ref[...] if final else x


def _combine(x1, ys, dest, route, mod, final_norm, nct, tm, final):
    bsz, seq, d = x1.shape
    off = nct if final else 0
    tpb = seq // tm - off
    per_b = seq // tm

    def tok_tile(i):
        return (i // tpb) * per_b + off + i % tpb

    n_steps = bsz * tpb
    dest3 = dest.reshape(bsz * per_b, 1, tm * TOP_K)
    smem = lambda imap: pl.BlockSpec((1, 1, tm * TOP_K), imap, memory_space=pltpu.SMEM)
    tile = lambda w: pl.BlockSpec((1, tm, w), lambda i: (i // tpb, off + i % tpb, 0))
    return pl.pallas_call(
        functools.partial(_combine_kernel, final=final, tile=tm),
        grid=(n_steps,),
        in_specs=[smem(lambda i: (tok_tile(i), 0, 0)),
                  smem(lambda i: (tok_tile(jnp.minimum(i + 1, n_steps - 1)), 0, 0)),
                  tile(d), tile(ROUTE_W),
                  pl.BlockSpec((1, 1, d), lambda i: (jnp.where(off + i % tpb < nct, bsz, i // tpb), 0, 5)),
                  pl.BlockSpec((1, d), lambda i: (0, 0)),
                  pl.BlockSpec(memory_space=pl.ANY)],
        out_specs=pl.BlockSpec((1, tm, d), lambda i: (i // tpb, i % tpb, 0)),
        out_shape=jax.ShapeDtypeStruct((bsz, tpb * tm, d), F32),
        scratch_shapes=[pltpu.VMEM((2, TOP_K, tm, d), F32), pltpu.SemaphoreType.DMA((2,))],
        name="combine",
        compiler_params=_cparams(("arbitrary",)),
    )(dest3, dest3, x1, route, mod, final_norm, ys)


def _rope_tables(n_ctx, n_lat):
    t = jnp.arange(n_lat, dtype=jnp.int32)
    row = (t // GRID_W).astype(F32)
    col = (t % GRID_W).astype(F32)
    inv = 1.0 / (ROPE_BASE ** (jnp.arange(0, AX_DIM, 2, dtype=F32) / AX_DIM))
    lane = jnp.arange(LANES)
    within = lane % DIFF_DH
    pos = jnp.where((within < AX_DIM)[None, :], row[:, None], col[:, None])
    ang = pos * inv[within % (AX_DIM // 2)][None, :]
    sign = jnp.where((within % AX_DIM) < AX_DIM // 2, -1.0, 1.0)[None, :]
    cos = jnp.concatenate([jnp.ones((n_ctx, LANES), F32), jnp.cos(ang)], axis=0)
    sin = jnp.concatenate([jnp.zeros((n_ctx, LANES), F32), jnp.sin(ang) * sign], axis=0)
    return cos, sin


def _pack_layer(layer, w_in, w_out, wa2_f, ba_f, wa2_b, ba_b, pool_w, wg, bg, we, be):
    d = w_in.shape[1]
    wi = w_in[layer]
    o = 0
    parts = {}
    for name, size in (("qg", GLA_K), ("kg", GLA_K), ("vg", GLA_V), ("og", GLA_V), ("af", GATE_RANK),
                       ("ab", GATE_RANK), ("qd", DIFF_QK), ("kd", DIFF_QK), ("vd", DIFF_V), ("pl", POOL_W)):
        parts[name] = wi[:, o:o + size]
        o += size
    gate = jnp.concatenate([parts["af"], parts["ab"], jnp.zeros((d, LANES - 2 * GATE_RANK), F32)], axis=1)
    w_all = jnp.concatenate([parts[n] for n in ("qg", "kg", "vg", "og", "qd", "kd", "vd", "pl")] + [gate],
                            axis=1).astype(BF16)
    wa2 = jnp.zeros((LANES, 2 * GLA_K), F32)
    wa2 = wa2.at[0:GATE_RANK, 0:GLA_K].set(wa2_f[layer])
    wa2 = wa2.at[GATE_RANK:2 * GATE_RANK, GLA_K:].set(wa2_b[layer])
    ba = jnp.concatenate([ba_f[layer], ba_b[layer]])[None, :]
    pw = jnp.zeros((POOL_W, POOL_W), F32)
    for gi in range(len(POOL_WINDOWS)):
        pw = pw.at[gi * POOL_CH:(gi + 1) * POOL_CH, gi * POOL_CH:(gi + 1) * POOL_CH].set(pool_w[layer, gi])
    w_route = jnp.concatenate([wg[layer], we[layer], jnp.zeros((d, ROUTE_W - N_GROUPS - N_EXPERTS), F32)], axis=1)
    b_route = jnp.concatenate([bg[layer], be[layer], jnp.zeros((ROUTE_W - N_GROUPS - N_EXPERTS,), F32)])[None, :]
    return dict(w_all=w_all, wa2=wa2, ba=ba, pool_w=pw.astype(BF16), w_out=w_out[layer].astype(BF16),
                w_route=w_route, b_route=b_route)


def kernel(x, c, ctx, c_ctx, w_mod, b_mod, norm1, norm2, w_in, w_out, gla_wa2_f, gla_ba_f, gla_wa2_b, gla_ba_b, gla_norm, lam_q1, lam_k1, lam_q2, lam_k2, diff_norm, pool_w, pool_scale, router_wg, router_bg, router_we, router_be, exp_w1, exp_w3, exp_w2, final_norm):
    bsz, n_lat, d = x.shape
    n_ctx = ctx.shape[1]
    depth = w_mod.shape[0]
    seq = n_ctx + n_lat
    tm = math.gcd(256, n_ctx)
    nct = n_ctx // tm
    tq = math.gcd(512, n_lat)
    tk = math.gcd(1024, n_lat // 2)
    gla_rows = math.gcd(64, n_ctx)
    assert bsz + 1 <= 8 and n_lat % tm == 0 and n_lat % GRID_W == 0

    cond = jnp.concatenate([c, c_ctx[None, :], jnp.zeros((8 - bsz - 1, d), F32)], axis=0)
    mod_all = _adaln(cond, w_mod, b_mod)
    cos_t, sin_t = _rope_tables(n_ctx, n_lat)
    xall = jnp.concatenate([ctx, x], axis=1)
    n_tok = bsz * seq

    for layer in range(depth):
        last = layer == depth - 1
        lam_init = 0.8 - 0.6 * math.exp(-0.3 * layer)
        prm = _pack_layer(layer, w_in, w_out, gla_wa2_f, gla_ba_f, gla_wa2_b, gla_ba_b, pool_w,
                          router_wg, router_bg, router_we, router_be)
        prm.update(gla_norm=gla_norm[layer][None, :], diff_norm=diff_norm[layer][None, :],
                   pool_scale=pool_scale[layer][None, :], norm2=norm2[layer][None, :])
        mod = mod_all[layer].reshape(8, 1, 6 * d)

        qg, kg, vg, og, la, qd, kd, vd, pool = _premix(
            xall, mod, norm1[layer][None, :], prm["w_all"], prm["wa2"], prm["ba"], cos_t, sin_t, nct, tm)
        o_f, o_b = _gla(qg, kg, vg, la, n_ctx, gla_rows)
        lam_vecs = [v[layer][None, :] for v in (lam_q1, lam_k1, lam_q2, lam_k2)]
        od_lat = _attention(qd[:, n_ctx:], kd, vd, lam_vecs, lam_init, n_lat, n_ctx, seq, tq, tk)
        od_ctx = _attention(qd, kd, vd, lam_vecs, lam_init, n_ctx, n_ctx, n_ctx, tm, tk)
        x1, h2p, route = _postmix(xall, o_f, o_b, og, od_ctx, od_lat, pool, mod, prm, nct, n_ctx, tm, lam_init)

        expert = route[..., 0:TOP_K].astype(jnp.int32).reshape(n_tok, TOP_K)
        dest, block_expert, n_slots = _slot_of_assignment(expert, n_tok)
        xs = _dispatch(h2p.reshape(n_tok, d // 2), dest, n_slots, tm)
        ys = _experts(xs, block_expert, exp_w1[layer].astype(BF16), exp_w3[layer].astype(BF16),
                      exp_w2[layer].astype(BF16))
        xall = _combine(x1, ys, dest, route, mod, final_norm[None, :], nct, tm, last)
    return xall
```

```python
import functools
import math

import jax
import jax.numpy as jnp
from jax import lax
from jax.experimental import pallas as pl
from jax.experimental.pallas import tpu as pltpu

F32 = jnp.float32
BF16 = jnp.bfloat16
HIGHEST = lax.Precision.HIGHEST

EPS = 1e-6
GRID_W = 64
GLA_HEADS, GLA_DK, GLA_DV = 4, 32, 64
GLA_K, GLA_V = GLA_HEADS * GLA_DK, GLA_HEADS * GLA_DV
GATE_RANK, GATE_TEMP = 16, 16.0
DIFF_HEADS, DIFF_DH = 4, 64
DIFF_DV = 2 * DIFF_DH
DIFF_QK = DIFF_HEADS * 2 * DIFF_DH
DIFF_V = DIFF_HEADS * DIFF_DV
Q_SCALE = DIFF_DH ** -0.5 * math.log2(math.e)
ROPE_BASE = 10000.0
AX_DIM = DIFF_DH // 2
POOL_WINDOWS = (2, 4, 8, 16)
POOL_CH = 64
POOL_W = len(POOL_WINDOWS) * POOL_CH
POOL_HALO = 8
N_GROUPS, EXPERTS_PER_GROUP = 4, 4
N_EXPERTS = N_GROUPS * EXPERTS_PER_GROUP
TOP_K = 2
MOE_BLOCK = 256

LANES = 128
SUB = 16
ROUTE_W = LANES
VMEM_LIMIT = 56 * 1024 * 1024


def _cparams(sem):
    return pltpu.CompilerParams(dimension_semantics=sem, vmem_limit_bytes=VMEM_LIMIT)


def _dot(a, b):
    return jnp.dot(a, b, preferred_element_type=F32)


def _dot_hi(a, b):
    return jnp.dot(a, b, precision=HIGHEST, preferred_element_type=F32)


def _dot_nt(a, b):
    return lax.dot_general(a, b, (((1,), (1,)), ((), ())), preferred_element_type=F32)


def _dot_tn(a, b):
    return lax.dot_general(a, b, (((0,), (0,)), ((), ())), preferred_element_type=F32)


def _silu(x):
    return x * jax.nn.sigmoid(x)


def _log_sigmoid(x):
    return jnp.minimum(x, 0.0) - jnp.log1p(jnp.exp(-jnp.abs(x)))


def _rms(x):
    return x * lax.rsqrt(jnp.mean(x * x, axis=-1, keepdims=True) + EPS)


def _adaln_kernel(c_ref, w_ref, b_ref, o_ref):
    o_ref[0] = _dot_hi(_silu(c_ref[...]), w_ref[0]) + b_ref[0]


def _adaln(cond, w_mod, b_mod):
    depth, d, six_d = w_mod.shape
    tn = 1536
    return pl.pallas_call(
        _adaln_kernel,
        grid=(depth, six_d // tn),
        in_specs=[pl.BlockSpec((8, d), lambda l, j: (0, 0)),
                  pl.BlockSpec((1, d, tn), lambda l, j: (l, 0, j)),
                  pl.BlockSpec((1, 1, tn), lambda l, j: (l, 0, j))],
        out_specs=pl.BlockSpec((1, 8, tn), lambda l, j: (l, 0, j)),
        out_shape=jax.ShapeDtypeStruct((depth, 8, six_d), F32),
        name="adaln",
        compiler_params=_cparams(("arbitrary", "arbitrary")),
    )(cond, w_mod, b_mod.reshape(depth, 1, six_d))


_C_QG, _C_KG, _C_VG, _C_OG = 0, GLA_K, 2 * GLA_K, 2 * GLA_K + GLA_V
_C_QD = 2 * GLA_K + 2 * GLA_V
_C_KD = _C_QD + DIFF_QK
_C_VD = _C_KD + DIFF_QK
_C_PL = _C_VD + DIFF_V
_C_GT = _C_PL + POOL_W
_C_END = _C_GT + LANES


def _premix_kernel(x_ref, sh_ref, sc_ref, n1_ref, w_ref, wa2_ref, ba_ref, cos_ref, sin_ref,
                   qg_ref, kg_ref, vg_ref, og_ref, la_ref, qd_ref, kd_ref, vd_ref, pool_ref):
    x = x_ref[0]
    h = _rms(x) * n1_ref[...]
    hb = (h * (1.0 + sc_ref[0]) + sh_ref[0]).astype(BF16)

    def proj(lo, hi):
        return _dot(hb, w_ref[:, lo:hi])

    qg_ref[0] = proj(_C_QG, _C_KG) * (GLA_DK ** -0.5)
    kg_ref[0] = proj(_C_KG, _C_VG)
    vg_ref[0] = proj(_C_VG, _C_OG).astype(BF16)
    og_ref[0] = proj(_C_OG, _C_QD)
    vd_ref[0] = proj(_C_VD, _C_PL).astype(BF16)
    pool_ref[0] = proj(_C_PL, _C_GT)
    pre = _dot_hi(proj(_C_GT, _C_END), wa2_ref[...]) + ba_ref[...]
    la_ref[0] = _log_sigmoid(pre) / GATE_TEMP

    cos = cos_ref[...]
    sin = sin_ref[...]
    lane = lax.broadcasted_iota(jnp.int32, (1, LANES), 1)
    first_half = (lane % AX_DIM) < (AX_DIM // 2)

    def rope(a):
        partner = jnp.where(first_half, pltpu.roll(a, LANES - AX_DIM // 2, 1), pltpu.roll(a, AX_DIM // 2, 1))
        return a * cos + partner * sin

    for hd in range(DIFF_HEADS):
        lo = hd * LANES
        qd_ref[0, :, lo:lo + LANES] = (rope(proj(_C_QD + lo, _C_QD + lo + LANES)) * Q_SCALE).astype(BF16)
        kd_ref[0, :, lo:lo + LANES] = rope(proj(_C_KD + lo, _C_KD + lo + LANES)).astype(BF16)


def _premix(xall, mod, norm1, w_all, wa2, ba, cos_t, sin_t, nct, tm):
    bsz, seq, d = xall.shape

    def mod_row(b, i):
        return jnp.where(i < nct, bsz, b)

    tile = lambda w: pl.BlockSpec((1, tm, w), lambda b, i: (b, i, 0))
    outs = [(GLA_K, F32), (GLA_K, F32), (GLA_V, BF16), (GLA_V, F32), (2 * GLA_K, F32),
            (DIFF_QK, BF16), (DIFF_QK, BF16), (DIFF_V, BF16), (POOL_W, F32)]
    return pl.pallas_call(
        _premix_kernel,
        grid=(bsz, seq // tm),
        in_specs=[tile(d),
                  pl.BlockSpec((1, 1, d), lambda b, i: (mod_row(b, i), 0, 0)),
                  pl.BlockSpec((1, 1, d), lambda b, i: (mod_row(b, i), 0, 1)),
                  pl.BlockSpec((1, d), lambda b, i: (0, 0)),
                  pl.BlockSpec((d, _C_END), lambda b, i: (0, 0)),
                  pl.BlockSpec((LANES, 2 * GLA_K), lambda b, i: (0, 0)),
                  pl.BlockSpec((1, 2 * GLA_K), lambda b, i: (0, 0)),
                  pl.BlockSpec((tm, LANES), lambda b, i: (i, 0)),
                  pl.BlockSpec((tm, LANES), lambda b, i: (i, 0))],
        out_specs=[tile(w) for w, _ in outs],
        out_shape=[jax.ShapeDtypeStruct((bsz, seq, w), dt) for w, dt in outs],
        name="premix",
        compiler_params=_cparams(("parallel", "parallel")),
    )(xall, mod, mod, norm1, w_all, wa2, ba, cos_t, sin_t)


def _gla_kernel(qf_ref, kf_ref, vf_ref, gf_ref, qb_ref, kb_ref, vb_ref, gb_ref,
                of_ref, ob_ref, st_ref, *, bsz, rows):
    @pl.when(pl.program_id(0) == 0)
    def _():
        st_ref[...] = jnp.zeros_like(st_ref)

    n_sub = rows // SUB
    rr = lax.broadcasted_iota(jnp.int32, (rows, rows), 0)
    cc = lax.broadcasted_iota(jnp.int32, (rows, rows), 1)
    same_sub = rr // SUB == cc // SUB
    tri = ((same_sub & (cc <= rr)).astype(F32), (same_sub & (cc >= rr)).astype(F32))
    t_idx = lax.broadcasted_iota(jnp.int32, (SUB, 1), 0)
    same_head = (lax.broadcasted_iota(jnp.int32, (GLA_V, GLA_K), 0) // GLA_DV
                 == lax.broadcasted_iota(jnp.int32, (GLA_V, GLA_K), 1) // GLA_DK)
    expand = (lax.broadcasted_iota(jnp.int32, (GLA_K, GLA_V), 0) // GLA_DK
              == lax.broadcasted_iota(jnp.int32, (GLA_K, GLA_V), 1) // GLA_DV).astype(BF16)
    chains = [(b, 0, qf_ref, kf_ref, vf_ref, gf_ref, of_ref) for b in range(bsz)]
    chains += [(b, 1, qb_ref, kb_ref, vb_ref, gb_ref, ob_ref) for b in range(bsz)]

    for b, rev, q_ref, k_ref, v_ref, g_ref, o_ref in chains:
        bloc_all = _dot_hi(tri[rev], g_ref[b])
        order = range(n_sub - 1, -1, -1) if rev else range(n_sub)
        parts = {}
        for j in order:
            rs = slice(j * SUB, (j + 1) * SUB)
            q = q_ref[b, rs, :]
            k = k_ref[b, rs, :]
            vb = v_ref[b, rs, :]
            v = vb.astype(F32)
            bloc = bloc_all[rs]
            bend = bloc[0:1] if rev else bloc[SUB - 1:SUB]
            qk_terms = []
            for s in range(SUB):
                valid = (t_idx <= s) if rev else (t_idx >= s)
                rel = jnp.where(valid, bloc - bloc[s:s + 1], -jnp.inf)
                qk_terms.append((q * k[s:s + 1] * jnp.exp(rel)).astype(BF16))
            att = _dot(jnp.concatenate(qk_terms, axis=0), expand)
            o_diag = att[0:SUB] * v[0:1]
            for s in range(1, SUB):
                o_diag = o_diag + att[s * SUB:(s + 1) * SUB] * v[s:s + 1]
            upd = _dot_tn(vb, (k * jnp.exp(bend - bloc)).astype(BF16))
            parts[j] = ((q * jnp.exp(bloc)).astype(BF16), o_diag, jnp.exp(bend), jnp.where(same_head, upd, 0.0))
        st = st_ref[2 * b + rev]
        for j in order:
            q_dec, o_diag, decay, upd = parts[j]
            o_ref[b, j * SUB:(j + 1) * SUB, :] = _dot_nt(q_dec, st.astype(BF16)) + o_diag
            st = st * decay + upd
        st_ref[2 * b + rev] = st


def _gla(qg, kg, vg, la, n_ctx, rows):
    bsz, seq, _ = qg.shape
    nc = n_ctx // rows
    nblk = seq // rows

    def fwd(i):
        return (0, i, 0)

    def bwd_blk(i):
        return jnp.where(i < nc, nc - 1 - i, nblk + nc - 1 - i)

    def bwd(i):
        return (0, bwd_blk(i), 0)

    def spec(w, imap):
        return pl.BlockSpec((bsz, rows, w), imap)

    return pl.pallas_call(
        functools.partial(_gla_kernel, bsz=bsz, rows=rows),
        grid=(nblk,),
        in_specs=[spec(GLA_K, fwd), spec(GLA_K, fwd), spec(GLA_V, fwd), spec(GLA_K, fwd),
                  spec(GLA_K, bwd), spec(GLA_K, bwd), spec(GLA_V, bwd),
                  spec(GLA_K, lambda i: (0, bwd_blk(i), 1))],
        out_specs=[spec(GLA_V, fwd), spec(GLA_V, bwd)],
        out_shape=[jax.ShapeDtypeStruct((bsz, seq, GLA_V), F32)] * 2,
        scratch_shapes=[pltpu.VMEM((2 * bsz, GLA_V, GLA_K), F32)],
        name="gla_scan",
        compiler_params=_cparams(("arbitrary",)),
    )(qg, kg, vg, la, qg, kg, vg, la)


def _attn_kernel(lq1_ref, lk1_ref, lq2_ref, lk2_ref, q_ref, k_ref, v_ref, o_ref,
                 m_ref, acc_ref, sa_ref, sb_ref, ma_ref, mb_ref, *, n_ctx, n_chunks, tk, lam_init):
    lam = (jnp.exp(jnp.sum(lq1_ref[...] * lk1_ref[...], axis=-1, keepdims=True))
           - jnp.exp(jnp.sum(lq2_ref[...] * lk2_ref[...], axis=-1, keepdims=True)) + lam_init)
    q = q_ref[0]
    lane = lax.broadcasted_iota(jnp.int32, (1, LANES), 1)
    zero = jnp.zeros_like(q)
    qs = (jnp.where(lane < DIFF_DH, q, zero), jnp.where(lane >= DIFF_DH, q, zero))
    m_ref[...] = jnp.full_like(m_ref, -jnp.inf)
    acc_ref[...] = jnp.zeros_like(acc_ref)

    def scores(start, size, s_ref, mx_ref):
        k = k_ref[0, pl.ds(start, size), :]
        for mp in range(2):
            s = _dot_nt(qs[mp], k)
            s_ref[mp, :, 0:size] = s
            mx_ref[mp] = jnp.max(s, axis=-1, keepdims=True)

    def consume(start, size, s_ref, mx_ref):
        v_ext = jnp.concatenate([v_ref[0, pl.ds(start, size), :], jnp.ones((size, LANES), BF16)], axis=1)
        for mp in range(2):
            m_old = m_ref[mp]
            m_new = jnp.maximum(m_old, mx_ref[mp])
            p = jnp.exp2(s_ref[mp, :, 0:size] - m_new).astype(BF16)
            acc_ref[mp] = jnp.exp2(m_old - m_new) * acc_ref[mp] + _dot(p, v_ext)
            m_ref[mp] = m_new

    def latent(j):
        return pl.multiple_of(n_ctx + j * tk, math.gcd(n_ctx, tk))

    buf_a, buf_b = (sa_ref, ma_ref), (sb_ref, mb_ref)
    scores(0, n_ctx, *buf_a)
    if n_chunks == 0:
        consume(0, n_ctx, *buf_a)
    else:
        scores(latent(0), tk, *buf_b)
        consume(0, n_ctx, *buf_a)

        def chunk_pair(jj, carry):
            scores(latent(2 * jj + 1), tk, *buf_a)
            consume(latent(2 * jj), tk, *buf_b)
            scores(latent(jnp.minimum(2 * jj + 2, n_chunks - 1)), tk, *buf_b)
            consume(latent(2 * jj + 1), tk, *buf_a)
            return carry

        lax.fori_loop(0, n_chunks // 2, chunk_pair, 0)
    o_ref[0] = (acc_ref[0, :, 0:LANES] / acc_ref[0, :, LANES:]
                - lam * (acc_ref[1, :, 0:LANES] / acc_ref[1, :, LANES:]))


def _attention(qd, kd, vd, lam_vecs, lam_init, n_q, n_ctx, n_kv, tq, tk):
    bsz = qd.shape[0]
    n_chunks = (n_kv - n_ctx) // tk
    assert n_chunks % 2 == 0 and n_ctx + n_chunks * tk == n_kv and n_q % tq == 0
    lam_spec = pl.BlockSpec((1, DIFF_DH), lambda b, h, i: (0, 0))
    return pl.pallas_call(
        functools.partial(_attn_kernel, n_ctx=n_ctx, n_chunks=n_chunks, tk=tk, lam_init=lam_init),
        grid=(bsz, DIFF_HEADS, n_q // tq),
        in_specs=[lam_spec] * 4 + [
            pl.BlockSpec((1, tq, LANES), lambda b, h, i: (b, i, h)),
            pl.BlockSpec((1, n_kv, LANES), lambda b, h, i: (b, 0, h)),
            pl.BlockSpec((1, n_kv, LANES), lambda b, h, i: (b, 0, h))],
        out_specs=pl.BlockSpec((1, tq, LANES), lambda b, h, i: (b, i, h)),
        out_shape=jax.ShapeDtypeStruct((bsz, n_q, DIFF_V), F32),
        scratch_shapes=[pltpu.VMEM((2, tq, 1), F32), pltpu.VMEM((2, tq, 2 * LANES), F32),
                        pltpu.VMEM((2, tq, max(tk, n_ctx)), F32), pltpu.VMEM((2, tq, max(tk, n_ctx)), F32),
                        pltpu.VMEM((2, tq, 1), F32), pltpu.VMEM((2, tq, 1), F32)],
        name="diff_attn",
        compiler_params=_cparams(("parallel", "parallel", "arbitrary")),
    )(*lam_vecs, qd, kd, vd)


def _postmix_kernel(x_ref, of_ref, ob_ref, og_ref, odc_ref, odl_ref, pc_ref, pp_ref, pn_ref,
                    gn_ref, dn_ref, pw_ref, ps_ref, wo_ref, g1_ref, sh2_ref, sc2_ref, n2_ref,
                    wr_ref, br_ref, x1_ref, h2_ref, route_ref, ext_ref,
                    *, tm, nct, n_ctx, seq, lam_init):
    i = pl.program_id(1)

    a = of_ref[0] + ob_ref[0]
    avg = (lax.broadcasted_iota(jnp.int32, (GLA_V, GLA_V), 0) // GLA_DV
           == lax.broadcasted_iota(jnp.int32, (GLA_V, GLA_V), 1) // GLA_DV).astype(F32) * (1.0 / GLA_DV)
    gla = a * lax.rsqrt(_dot_hi(a * a, avg) + EPS) * gn_ref[...] * _silu(og_ref[0])
    y = _dot(gla.astype(BF16), wo_ref[0:GLA_V, :])

    for hd in range(DIFF_HEADS):
        lo = hd * DIFF_DV
        od = jnp.where(i < nct, odc_ref[0, :, lo:lo + DIFF_DV], odl_ref[0, :, lo:lo + DIFF_DV])
        dh = _rms(od) * dn_ref[:, lo:lo + DIFF_DV] * (1.0 - lam_init)
        y = y + _dot(dh.astype(BF16), wo_ref[GLA_V + lo:GLA_V + lo + DIFF_DV, :])

    seg_lo = jnp.where(i < nct, 0, n_ctx)
    seg_hi = jnp.where(i < nct, n_ctx, seq)
    ext_ref[0:POOL_HALO] = pp_ref[0]
    ext_ref[POOL_HALO:POOL_HALO + tm] = pc_ref[0]
    ext_ref[POOL_HALO + tm:] = pn_ref[0]
    pos_e = i * tm - POOL_HALO + lax.broadcasted_iota(jnp.int32, (tm + 2 * POOL_HALO, 1), 0)
    e = jnp.where((pos_e >= seg_lo) & (pos_e < seg_hi), ext_ref[...], 0.0)
    pos = i * tm + lax.broadcasted_iota(jnp.int32, (tm, 1), 0)
    grp = lax.broadcasted_iota(jnp.int32, (1, POOL_W), 1) // POOL_CH
    run, width, mean = e, 1, jnp.zeros((tm, POOL_W), F32)
    for gi, w in enumerate(POOL_WINDOWS):
        while width < w:
            n = run.shape[0] - width
            run = run[0:n] + run[width:width + n]
            width *= 2
        start = POOL_HALO - w // 2
        cnt = (jnp.minimum(pos + (w - w // 2), seg_hi) - jnp.maximum(pos - w // 2, seg_lo)).astype(F32)
        mean = jnp.where(grp == gi, run[start:start + tm] / cnt, mean)
    pooled = _dot((mean - pc_ref[0]).astype(BF16), pw_ref[...]) * ps_ref[...]
    y = y + _dot(pooled.astype(BF16), wo_ref[GLA_V + DIFF_V:, :])

    x1 = x_ref[0] + g1_ref[0] * y
    x1_ref[0] = x1
    h2 = _rms(x1) * n2_ref[...] * (1.0 + sc2_ref[0]) + sh2_ref[0]
    bits = lax.bitcast_convert_type(h2.astype(BF16).astype(F32), jnp.uint32)
    half = bits.shape[1] // 2
    h2_ref[0] = (bits[:, 0:half] >> 16) | (bits[:, half:] & jnp.uint32(0xFFFF0000))

    logit = _dot_hi(h2, wr_ref[...]) + br_ref[...]
    lane = lax.broadcasted_iota(jnp.int32, (1, ROUTE_W), 1).astype(F32)
    neg = -jnp.inf

    def top(vals):
        mx = jnp.max(vals, axis=-1, keepdims=True)
        idx = jnp.min(jnp.where(vals == mx, lane, float(ROUTE_W)), axis=-1, keepdims=True)
        return mx, idx

    gl = jnp.where(lane < N_GROUPS, logit, neg)
    gmax, gidx = top(gl)
    g_top = 1.0 / jnp.sum(jnp.exp(gl - gmax), axis=-1, keepdims=True)
    e_lo = N_GROUPS + gidx * EXPERTS_PER_GROUP
    el = jnp.where((lane >= e_lo) & (lane < e_lo + EXPERTS_PER_GROUP), logit, neg)
    emax, idx1 = top(el)
    esum = jnp.sum(jnp.exp(el - emax), axis=-1, keepdims=True)
    emax2, idx2 = top(jnp.where(lane == idx1, neg, el))
    e1 = 1.0 / esum
    e2 = jnp.exp(emax2 - emax) / esum
    w1 = g_top * e1 / (e1 + e2)
    w2 = g_top * e2 / (e1 + e2)
    rec = jnp.where(lane == 0, idx1 - N_GROUPS, 0.0)
    rec = jnp.where(lane == 1, idx2 - N_GROUPS, rec)
    rec = jnp.where(lane == 2, w1, rec)
    route_ref[0] = jnp.where(lane == 3, w2, rec)


def _postmix(xall, o_f, o_b, og, od_ctx, od_lat, pool, mod, prm, nct, n_ctx, tm, lam_init):
    bsz, seq, d = xall.shape
    hpb = tm // POOL_HALO
    n_halo = seq // POOL_HALO

    def mod_spec(col):
        return pl.BlockSpec((1, 1, d), lambda b, i: (jnp.where(i < nct, bsz, b), 0, col))

    tile = lambda w: pl.BlockSpec((1, tm, w), lambda b, i: (b, i, 0))
    full = lambda r, c: pl.BlockSpec((r, c), lambda b, i: (0, 0))
    return pl.pallas_call(
        functools.partial(_postmix_kernel, tm=tm, nct=nct, n_ctx=n_ctx, seq=seq, lam_init=lam_init),
        grid=(bsz, seq // tm),
        in_specs=[tile(d), tile(GLA_V), tile(GLA_V), tile(GLA_V),
                  pl.BlockSpec((1, tm, DIFF_V), lambda b, i: (b, jnp.minimum(i, nct - 1), 0)),
                  pl.BlockSpec((1, tm, DIFF_V), lambda b, i: (b, jnp.maximum(i - nct, 0), 0)),
                  tile(POOL_W),
                  pl.BlockSpec((1, POOL_HALO, POOL_W), lambda b, i: (b, jnp.maximum(i * hpb - 1, 0), 0)),
                  pl.BlockSpec((1, POOL_HALO, POOL_W), lambda b, i: (b, jnp.minimum((i + 1) * hpb, n_halo - 1), 0)),
                  full(1, GLA_V), full(1, DIFF_V), full(POOL_W, POOL_W), full(1, POOL_W), full(d, d),
                  mod_spec(2), mod_spec(3), mod_spec(4), full(1, d), full(d, ROUTE_W), full(1, ROUTE_W)],
        out_specs=[tile(d), tile(d // 2), tile(ROUTE_W)],
        out_shape=[jax.ShapeDtypeStruct((bsz, seq, d), F32), jax.ShapeDtypeStruct((bsz, seq, d // 2), jnp.uint32),
                   jax.ShapeDtypeStruct((bsz, seq, ROUTE_W), F32)],
        scratch_shapes=[pltpu.VMEM((tm + 2 * POOL_HALO, POOL_W), F32)],
        name="postmix",
        compiler_params=_cparams(("parallel", "parallel")),
    )(xall, o_f, o_b, og, od_ctx, od_lat, pool, pool, pool, prm["gla_norm"], prm["diff_norm"], prm["pool_w"],
      prm["pool_scale"], prm["w_out"], mod, mod, mod, prm["norm2"], prm["w_route"], prm["b_route"])


def _slot_of_assignment(expert, n_tok):
    n_assign = n_tok * TOP_K
    e = expert.reshape(n_assign)
    hot = (e[:, None] == jnp.arange(N_EXPERTS, dtype=jnp.int32)[None, :]).astype(jnp.int32)
    csum = jnp.cumsum(hot, axis=0)
    counts = csum[-1]
    rank = jnp.sum(csum * hot, axis=1) - 1
    padded = (counts + MOE_BLOCK - 1) // MOE_BLOCK * MOE_BLOCK
    padded_end = jnp.cumsum(padded)
    dest = (padded_end - padded)[e] + rank
    n_slots = -(-n_assign // MOE_BLOCK) * MOE_BLOCK + N_EXPERTS * MOE_BLOCK
    block_start = jnp.arange(n_slots // MOE_BLOCK, dtype=jnp.int32) * MOE_BLOCK
    block_expert = jnp.minimum(jnp.sum((padded_end[None, :] <= block_start[:, None]).astype(jnp.int32), axis=1),
                               N_EXPERTS - 1)
    return dest, block_expert, n_slots


def _each(n, fn):
    def step(r, carry):
        fn(r)
        return carry
    lax.fori_loop(0, n, step, 0, unroll=8)


def _dispatch_kernel(dcur_ref, dprev_ref, h_ref, xs_in, xs_hbm, buf, sem, *, tile):
    del xs_in
    i = pl.program_id(0)
    n_tiles = pl.num_programs(0) - 1

    def copy(step, dest_ref, r, k):
        return pltpu.make_async_copy(buf.at[step % 2, pl.ds(r, 1), :],
                                     xs_hbm.at[pl.ds(dest_ref[0, 0, r * TOP_K + k], 1), :], sem.at[step % 2])

    @pl.when(i < n_tiles)
    def _():
        buf[i % 2] = h_ref[...]
        _each(tile, lambda r: [copy(i, dcur_ref, r, k).start() for k in range(TOP_K)])

    @pl.when(i > 0)
    def _():
        _each(tile, lambda r: [copy(i - 1, dprev_ref, r, k).wait() for k in range(TOP_K)])


def _dispatch(h2p, dest, n_slots, tile):
    n_tok, w = h2p.shape
    n_tiles = n_tok // tile
    dest3 = dest.reshape(n_tiles, 1, tile * TOP_K)
    smem = lambda imap: pl.BlockSpec((1, 1, tile * TOP_K), imap, memory_space=pltpu.SMEM)
    return pl.pallas_call(
        functools.partial(_dispatch_kernel, tile=tile),
        grid=(n_tiles + 1,),
        in_specs=[smem(lambda i: (jnp.minimum(i, n_tiles - 1), 0, 0)), smem(lambda i: (jnp.maximum(i - 1, 0), 0, 0)),
                  pl.BlockSpec((tile, w), lambda i: (jnp.minimum(i, n_tiles - 1), 0)),
                  pl.BlockSpec(memory_space=pl.ANY)],
        out_specs=pl.BlockSpec(memory_space=pl.ANY),
        out_shape=jax.ShapeDtypeStruct((n_slots, w), jnp.uint32),
        scratch_shapes=[pltpu.VMEM((2, tile, w), jnp.uint32), pltpu.SemaphoreType.DMA((2,))],
        input_output_aliases={3: 0},
        name="moe_dispatch",
        compiler_params=_cparams(("arbitrary",)),
    )(dest3, dest3, h2p, jnp.zeros((n_slots, w), jnp.uint32))


def _expert_kernel(be_ref, xs_ref, w1_ref, w3_ref, w2_ref, ys_ref):
    del be_ref
    bits = xs_ref[...]
    half = bits.shape[1]
    x_lo = lax.bitcast_convert_type(bits << 16, F32).astype(BF16)
    x_hi = lax.bitcast_convert_type(bits & jnp.uint32(0xFFFF0000), F32).astype(BF16)

    def up(w_ref):
        return _dot(x_lo, w_ref[0, 0:half, :]) + _dot(x_hi, w_ref[0, half:, :])

    ys_ref[...] = _dot((_silu(up(w1_ref)) * up(w3_ref)).astype(BF16), w2_ref[0])


def _experts(xs, block_expert, w1, w3, w2):
    n_slots, half = xs.shape
    _, d, d_exp = w1.shape
    return pl.pallas_call(
        _expert_kernel,
        grid_spec=pltpu.PrefetchScalarGridSpec(
            num_scalar_prefetch=1,
            grid=(n_slots // MOE_BLOCK,),
            in_specs=[pl.BlockSpec((MOE_BLOCK, half), lambda i, be: (i, 0)),
                      pl.BlockSpec((1, d, d_exp), lambda i, be: (be[i], 0, 0)),
                      pl.BlockSpec((1, d, d_exp), lambda i, be: (be[i], 0, 0)),
                      pl.BlockSpec((1, d_exp, d), lambda i, be: (be[i], 0, 0))],
            out_specs=pl.BlockSpec((MOE_BLOCK, d), lambda i, be: (i, 0))),
        out_shape=jax.ShapeDtypeStruct((n_slots, d), F32),
        name="moe_experts",
        compiler_params=_cparams(("arbitrary",)),
    )(block_expert, xs, w1, w3, w2)


def _combine_kernel(dcur_ref, dnext_ref, x1_ref, route_ref, g2_ref, fn_ref, ys_hbm, o_ref, ybuf, sem,
                    *, final, tile):
    i = pl.program_id(0)
    n = pl.num_programs(0)

    def fetch(step, dest_ref, r, k):
        slot = step % 2
        return pltpu.make_async_copy(ys_hbm.at[pl.ds(dest_ref[0, 0, r * TOP_K + k], 1), :],
                                     ybuf.at[slot, k, pl.ds(r, 1), :], sem.at[slot])

    @pl.when(i == 0)
    def _():
        _each(tile, lambda r: [fetch(i, dcur_ref, r, k).start() for k in range(TOP_K)])

    @pl.when(i + 1 < n)
    def _():
        _each(tile, lambda r: [fetch(i + 1, dnext_ref, r, k).start() for k in range(TOP_K)])

    _each(tile, lambda r: [fetch(i, dcur_ref, r, k).wait() for k in range(TOP_K)])
    route = route_ref[0]
    lane = lax.broadcasted_iota(jnp.int32, (1, ROUTE_W), 1)
    w0 = jnp.sum(jnp.where(lane == 2, route, 0.0), axis=-1, keepdims=True)
    w1 = jnp.sum(jnp.where(lane == 3, route, 0.0), axis=-1, keepdims=True)
    x = x1_ref[0] + g2_ref[0] * (ybuf[i % 2, 0] * w0 + ybuf[i % 2, 1] * w1)
    o_ref[0] = _rms(x) * fn_ref[...] if final else x


def _combine(x1, ys, dest, route, mod, final_norm, nct, tm, final):
    bsz, seq, d = x1.shape
    off = nct if final else 0
    tpb = seq // tm - off
    per_b = seq // tm

    def tok_tile(i):
        return (i // tpb) * per_b + off + i % tpb

    n_steps = bsz * tpb
    dest3 = dest.reshape(bsz * per_b, 1, tm * TOP_K)
    smem = lambda imap: pl.BlockSpec((1, 1, tm * TOP_K), imap, memory_space=pltpu.SMEM)
    tile = lambda w: pl.BlockSpec((1, tm, w), lambda i: (i // tpb, off + i % tpb, 0))
    return pl.pallas_call(
        functools.partial(_combine_kernel, final=final, tile=tm),
        grid=(n_steps,),
        in_specs=[smem(lambda i: (tok_tile(i), 0, 0)),
                  smem(lambda i: (tok_tile(jnp.minimum(i + 1, n_steps - 1)), 0, 0)),
                  tile(d), tile(ROUTE_W),
                  pl.BlockSpec((1, 1, d), lambda i: (jnp.where(off + i % tpb < nct, bsz, i // tpb), 0, 5)),
                  pl.BlockSpec((1, d), lambda i: (0, 0)),
                  pl.BlockSpec(memory_space=pl.ANY)],
        out_specs=pl.BlockSpec((1, tm, d), lambda i: (i // tpb, i % tpb, 0)),
        out_shape=jax.ShapeDtypeStruct((bsz, tpb * tm, d), F32),
        scratch_shapes=[pltpu.VMEM((2, TOP_K, tm, d), F32), pltpu.SemaphoreType.DMA((2,))],
        name="combine",
        compiler_params=_cparams(("arbitrary",)),
    )(dest3, dest3, x1, route, mod, final_norm, ys)


def _rope_tables(n_ctx, n_lat):
    t = jnp.arange(n_lat, dtype=jnp.int32)
    row = (t // GRID_W).astype(F32)
    col = (t % GRID_W).astype(F32)
    inv = 1.0 / (ROPE_BASE ** (jnp.arange(0, AX_DIM, 2, dtype=F32) / AX_DIM))
    lane = jnp.arange(LANES)
    within = lane % DIFF_DH
    pos = jnp.where((within < AX_DIM)[None, :], row[:, None], col[:, None])
    ang = pos * inv[within % (AX_DIM // 2)][None, :]
    sign = jnp.where((within % AX_DIM) < AX_DIM // 2, -1.0, 1.0)[None, :]
    cos = jnp.concatenate([jnp.ones((n_ctx, LANES), F32), jnp.cos(ang)], axis=0)
    sin = jnp.concatenate([jnp.zeros((n_ctx, LANES), F32), jnp.sin(ang) * sign], axis=0)
    return cos, sin


def _pack_layer(layer, w_in, w_out, wa2_f, ba_f, wa2_b, ba_b, pool_w, wg, bg, we, be):
    d = w_in.shape[1]
    wi = w_in[layer]
    o = 0
    parts = {}
    for name, size in (("qg", GLA_K), ("kg", GLA_K), ("vg", GLA_V), ("og", GLA_V), ("af", GATE_RANK),
                       ("ab", GATE_RANK), ("qd", DIFF_QK), ("kd", DIFF_QK), ("vd", DIFF_V), ("pl", POOL_W)):
        parts[name] = wi[:, o:o + size]
        o += size
    gate = jnp.concatenate([parts["af"], parts["ab"], jnp.zeros((d, LANES - 2 * GATE_RANK), F32)], axis=1)
    w_all = jnp.concatenate([parts[n] for n in ("qg", "kg", "vg", "og", "qd", "kd", "vd", "pl")] + [gate],
                            axis=1).astype(BF16)
    wa2 = jnp.zeros((LANES, 2 * GLA_K), F32)
    wa2 = wa2.at[0:GATE_RANK, 0:GLA_K].set(wa2_f[layer])
    wa2 = wa2.at[GATE_RANK:2 * GATE_RANK, GLA_K:].set(wa2_b[layer])
    ba = jnp.concatenate([ba_f[layer], ba_b[layer]])[None, :]
    pw = jnp.zeros((POOL_W, POOL_W), F32)
    for gi in range(len(POOL_WINDOWS)):
        pw = pw.at[gi * POOL_CH:(gi + 1) * POOL_CH, gi * POOL_CH:(gi + 1) * POOL_CH].set(pool_w[layer, gi])
    w_route = jnp.concatenate([wg[layer], we[layer], jnp.zeros((d, ROUTE_W - N_GROUPS - N_EXPERTS), F32)], axis=1)
    b_route = jnp.concatenate([bg[layer], be[layer], jnp.zeros((ROUTE_W - N_GROUPS - N_EXPERTS,), F32)])[None, :]
    return dict(w_all=w_all, wa2=wa2, ba=ba, pool_w=pw.astype(BF16), w_out=w_out[layer].astype(BF16),
                w_route=w_route, b_route=b_route)


def kernel(x, c, ctx, c_ctx, w_mod, b_mod, norm1, norm2, w_in, w_out, gla_wa2_f, gla_ba_f, gla_wa2_b, gla_ba_b, gla_norm, lam_q1, lam_k1, lam_q2, lam_k2, diff_norm, pool_w, pool_scale, router_wg, router_bg, router_we, router_be, exp_w1, exp_w3, exp_w2, final_norm):
    bsz, n_lat, d = x.shape
    n_ctx = ctx.shape[1]
    depth = w_mod.shape[0]
    seq = n_ctx + n_lat
    tm = math.gcd(256, n_ctx)
    nct = n_ctx // tm
    tq = math.gcd(512, n_lat)
    tk = math.gcd(1024, n_lat // 2)
    gla_rows = math.gcd(64, n_ctx)
    assert bsz + 1 <= 8 and n_lat % tm == 0 and n_lat % GRID_W == 0

    cond = jnp.concatenate([c, c_ctx[None, :], jnp.zeros((8 - bsz - 1, d), F32)], axis=0)
    mod_all = _adaln(cond, w_mod, b_mod)
    cos_t, sin_t = _rope_tables(n_ctx, n_lat)
    xall = jnp.concatenate([ctx, x], axis=1)
    n_tok = bsz * seq

    for layer in range(depth):
        last = layer == depth - 1
        lam_init = 0.8 - 0.6 * math.exp(-0.3 * layer)
        prm = _pack_layer(layer, w_in, w_out, gla_wa2_f, gla_ba_f, gla_wa2_b, gla_ba_b, pool_w,
                          router_wg, router_bg, router_we, router_be)
        prm.update(gla_norm=gla_norm[layer][None, :], diff_norm=diff_norm[layer][None, :],
                   pool_scale=pool_scale[layer][None, :], norm2=norm2[layer][None, :])
        mod = mod_all[layer].reshape(8, 1, 6 * d)

        qg, kg, vg, og, la, qd, kd, vd, pool = _premix(
            xall, mod, norm1[layer][None, :], prm["w_all"], prm["wa2"], prm["ba"], cos_t, sin_t, nct, tm)
        o_f, o_b = _gla(qg, kg, vg, la, n_ctx, gla_rows)
        lam_vecs = [v[layer][None, :] for v in (lam_q1, lam_k1, lam_q2, lam_k2)]
        od_lat = _attention(qd[:, n_ctx:], kd, vd, lam_vecs, lam_init, n_lat, n_ctx, seq, tq, tk)
        od_ctx = _attention(qd, kd, vd, lam_vecs, lam_init, n_ctx, n_ctx, n_ctx, tm, tk)
        x1, h2p, route = _postmix(xall, o_f, o_b, og, od_ctx, od_lat, pool, mod, prm, nct, n_ctx, tm, lam_init)

        expert = route[..., 0:TOP_K].astype(jnp.int32).reshape(n_tok, TOP_K)
        dest, block_expert, n_slots = _slot_of_assignment(expert, n_tok)
        xs = _dispatch(h2p.reshape(n_tok, d // 2), dest, n_slots, tm)
        ys = _experts(xs, block_expert, exp_w1[layer].astype(BF16), exp_w3[layer].astype(BF16),
                      exp_w2[layer].astype(BF16))
        xall = _combine(x1, ys, dest, route, mod, final_norm[None, :], nct, tm, last)
    return xall
```

```python
import functools
import math

import jax
import jax.numpy as jnp
from jax import lax
from jax.experimental import pallas as pl
from jax.experimental.pallas import tpu as pltpu

F32 = jnp.float32
BF16 = jnp.bfloat16
HIGHEST = lax.Precision.HIGHEST

EPS = 1e-6
GRID_W = 64
GLA_HEADS, GLA_DK, GLA_DV = 4, 32, 64
GLA_K, GLA_V = GLA_HEADS * GLA_DK, GLA_HEADS * GLA_DV
GATE_RANK, GATE_TEMP = 16, 16.0
DIFF_HEADS, DIFF_DH = 4, 64
DIFF_DV = 2 * DIFF_DH
DIFF_QK = DIFF_HEADS * 2 * DIFF_DH
DIFF_V = DIFF_HEADS * DIFF_DV
Q_SCALE = DIFF_DH ** -0.5 * math.log2(math.e)
ROPE_BASE = 10000.0
AX_DIM = DIFF_DH // 2
POOL_WINDOWS = (2, 4, 8, 16)
POOL_CH = 64
POOL_W = len(POOL_WINDOWS) * POOL_CH
POOL_HALO = 8
N_GROUPS, EXPERTS_PER_GROUP = 4, 4
N_EXPERTS = N_GROUPS * EXPERTS_PER_GROUP
TOP_K = 2
MOE_BLOCK = 256

LANES = 128
SUB = 16
ROUTE_W = LANES
VMEM_LIMIT = 56 * 1024 * 1024


def _cparams(sem):
    return pltpu.CompilerParams(dimension_semantics=sem, vmem_limit_bytes=VMEM_LIMIT)


def _dot(a, b):
    return jnp.dot(a, b, preferred_element_type=F32)


def _dot_hi(a, b):
    return jnp.dot(a, b, precision=HIGHEST, preferred_element_type=F32)


def _dot_nt(a, b):
    return lax.dot_general(a, b, (((1,), (1,)), ((), ())), preferred_element_type=F32)


def _dot_tn(a, b):
    return lax.dot_general(a, b, (((0,), (0,)), ((), ())), preferred_element_type=F32)


def _split_bf16(x):
    hi = x.astype(BF16)
    return hi, (x - hi.astype(F32)).astype(BF16)


def _silu(x):
    return x * jax.nn.sigmoid(x)


def _log_sigmoid(x):
    return jnp.minimum(x, 0.0) - jnp.log1p(jnp.exp(-jnp.abs(x)))


def _rms(x):
    return x * lax.rsqrt(jnp.mean(x * x, axis=-1, keepdims=True) + EPS)


def _adaln_kernel(c_ref, w_ref, b_ref, o_ref):
    o_ref[0] = _dot_hi(_silu(c_ref[...]), w_ref[0]) + b_ref[0]


def _adaln(cond, w_mod, b_mod):
    depth, d, six_d = w_mod.shape
    tn = 1536
    return pl.pallas_call(
        _adaln_kernel,
        grid=(depth, six_d // tn),
        in_specs=[pl.BlockSpec((8, d), lambda l, j: (0, 0)),
                  pl.BlockSpec((1, d, tn), lambda l, j: (l, 0, j)),
                  pl.BlockSpec((1, 1, tn), lambda l, j: (l, 0, j))],
        out_specs=pl.BlockSpec((1, 8, tn), lambda l, j: (l, 0, j)),
        out_shape=jax.ShapeDtypeStruct((depth, 8, six_d), F32),
        name="adaln",
        compiler_params=_cparams(("arbitrary", "arbitrary")),
    )(cond, w_mod, b_mod.reshape(depth, 1, six_d))


_C_QG, _C_KG, _C_VG, _C_OG = 0, GLA_K, 2 * GLA_K, 2 * GLA_K + GLA_V
_C_QD = 2 * GLA_K + 2 * GLA_V
_C_KD = _C_QD + DIFF_QK
_C_VD = _C_KD + DIFF_QK
_C_PL = _C_VD + DIFF_V
_C_GT = _C_PL + POOL_W
_C_END = _C_GT + LANES


def _premix_kernel(x_ref, sh_ref, sc_ref, n1_ref, w_ref, wa2_ref, ba_ref, cos_ref, sin_ref,
                   qg_ref, kg_ref, vg_ref, og_ref, la_ref, qd_ref, kd_ref, vd_ref, pool_ref):
    x = x_ref[0]
    h = _rms(x) * n1_ref[...]
    hb = (h * (1.0 + sc_ref[0]) + sh_ref[0]).astype(BF16)

    def proj(lo, hi):
        return _dot(hb, w_ref[:, lo:hi])

    qg_ref[0] = proj(_C_QG, _C_KG) * (GLA_DK ** -0.5)
    kg_ref[0] = proj(_C_KG, _C_VG)
    vg_ref[0] = proj(_C_VG, _C_OG).astype(BF16)
    og_ref[0] = proj(_C_OG, _C_QD)
    vd_ref[0] = proj(_C_VD, _C_PL).astype(BF16)
    pool_ref[0] = proj(_C_PL, _C_GT)
    pre = _dot_hi(proj(_C_GT, _C_END), wa2_ref[...]) + ba_ref[...]
    la_ref[0] = _log_sigmoid(pre) / GATE_TEMP

    cos = cos_ref[...]
    sin = sin_ref[...]
    lane = lax.broadcasted_iota(jnp.int32, (1, LANES), 1)
    first_half = (lane % AX_DIM) < (AX_DIM // 2)

    def rope(a):
        partner = jnp.where(first_half, pltpu.roll(a, LANES - AX_DIM // 2, 1), pltpu.roll(a, AX_DIM // 2, 1))
        return a * cos + partner * sin

    for hd in range(DIFF_HEADS):
        lo = hd * LANES
        qd_ref[0, :, lo:lo + LANES] = (rope(proj(_C_QD + lo, _C_QD + lo + LANES)) * Q_SCALE).astype(BF16)
        kd_ref[0, :, lo:lo + LANES] = rope(proj(_C_KD + lo, _C_KD + lo + LANES)).astype(BF16)


def _premix(xall, mod, norm1, w_all, wa2, ba, cos_t, sin_t, nct, tm):
    bsz, seq, d = xall.shape

    def mod_row(b, i):
        return jnp.where(i < nct, bsz, b)

    tile = lambda w: pl.BlockSpec((1, tm, w), lambda b, i: (b, i, 0))
    outs = [(GLA_K, F32), (GLA_K, F32), (GLA_V, BF16), (GLA_V, F32), (2 * GLA_K, F32),
            (DIFF_QK, BF16), (DIFF_QK, BF16), (DIFF_V, BF16), (POOL_W, F32)]
    return pl.pallas_call(
        _premix_kernel,
        grid=(bsz, seq // tm),
        in_specs=[tile(d),
                  pl.BlockSpec((1, 1, d), lambda b, i: (mod_row(b, i), 0, 0)),
                  pl.BlockSpec((1, 1, d), lambda b, i: (mod_row(b, i), 0, 1)),
                  pl.BlockSpec((1, d), lambda b, i: (0, 0)),
                  pl.BlockSpec((d, _C_END), lambda b, i: (0, 0)),
                  pl.BlockSpec((LANES, 2 * GLA_K), lambda b, i: (0, 0)),
                  pl.BlockSpec((1, 2 * GLA_K), lambda b, i: (0, 0)),
                  pl.BlockSpec((tm, LANES), lambda b, i: (i, 0)),
                  pl.BlockSpec((tm, LANES), lambda b, i: (i, 0))],
        out_specs=[tile(w) for w, _ in outs],
        out_shape=[jax.ShapeDtypeStruct((bsz, seq, w), dt) for w, dt in outs],
        name="premix",
        compiler_params=_cparams(("parallel", "parallel")),
    )(xall, mod, mod, norm1, w_all, wa2, ba, cos_t, sin_t)


def _gla_kernel(qf_ref, kf_ref, vf_ref, gf_ref, qb_ref, kb_ref, vb_ref, gb_ref,
                of_ref, ob_ref, st_ref, *, bsz, rows):
    @pl.when(pl.program_id(0) == 0)
    def _():
        st_ref[...] = jnp.zeros_like(st_ref)

    n_sub = rows // SUB
    rr = lax.broadcasted_iota(jnp.int32, (rows, rows), 0)
    cc = lax.broadcasted_iota(jnp.int32, (rows, rows), 1)
    same_sub = rr // SUB == cc // SUB
    tri = ((same_sub & (cc <= rr)).astype(F32), (same_sub & (cc >= rr)).astype(F32))
    t_idx = lax.broadcasted_iota(jnp.int32, (SUB, 1), 0)
    same_head = (lax.broadcasted_iota(jnp.int32, (GLA_V, GLA_K), 0) // GLA_DV
                 == lax.broadcasted_iota(jnp.int32, (GLA_V, GLA_K), 1) // GLA_DK)
    expand = (lax.broadcasted_iota(jnp.int32, (GLA_K, GLA_V), 0) // GLA_DK
              == lax.broadcasted_iota(jnp.int32, (GLA_K, GLA_V), 1) // GLA_DV).astype(BF16)
    chains = [(b, 0, qf_ref, kf_ref, vf_ref, gf_ref, of_ref) for b in range(bsz)]
    chains += [(b, 1, qb_ref, kb_ref, vb_ref, gb_ref, ob_ref) for b in range(bsz)]

    for b, rev, q_ref, k_ref, v_ref, g_ref, o_ref in chains:
        bloc_all = _dot_hi(tri[rev], g_ref[b])
        order = range(n_sub - 1, -1, -1) if rev else range(n_sub)
        parts = {}
        for j in order:
            rs = slice(j * SUB, (j + 1) * SUB)
            q = q_ref[b, rs, :]
            k = k_ref[b, rs, :]
            vb = v_ref[b, rs, :]
            v = vb.astype(F32)
            bloc = bloc_all[rs]
            bend = bloc[0:1] if rev else bloc[SUB - 1:SUB]
            qk_terms = []
            for s in range(SUB):
                valid = (t_idx <= s) if rev else (t_idx >= s)
                rel = jnp.where(valid, bloc - bloc[s:s + 1], -jnp.inf)
                qk_terms.append((q * k[s:s + 1] * jnp.exp(rel)).astype(BF16))
            att = _dot(jnp.concatenate(qk_terms, axis=0), expand)
            o_diag = att[0:SUB] * v[0:1]
            for s in range(1, SUB):
                o_diag = o_diag + att[s * SUB:(s + 1) * SUB] * v[s:s + 1]
            upd = _dot_tn(vb, (k * jnp.exp(bend - bloc)).astype(BF16))
            parts[j] = ((q * jnp.exp(bloc)).astype(BF16), o_diag, jnp.exp(bend), jnp.where(same_head, upd, 0.0))
        st = st_ref[2 * b + rev]
        for j in order:
            q_dec, o_diag, decay, upd = parts[j]
            o_ref[b, j * SUB:(j + 1) * SUB, :] = _dot_nt(q_dec, st.astype(BF16)) + o_diag
            st = st * decay + upd
        st_ref[2 * b + rev] = st


def _gla(qg, kg, vg, la, n_ctx, rows):
    bsz, seq, _ = qg.shape
    nc = n_ctx // rows
    nblk = seq // rows

    def fwd(i):
        return (0, i, 0)

    def bwd_blk(i):
        return jnp.where(i < nc, nc - 1 - i, nblk + nc - 1 - i)

    def bwd(i):
        return (0, bwd_blk(i), 0)

    def spec(w, imap):
        return pl.BlockSpec((bsz, rows, w), imap)

    return pl.pallas_call(
        functools.partial(_gla_kernel, bsz=bsz, rows=rows),
        grid=(nblk,),
        in_specs=[spec(GLA_K, fwd), spec(GLA_K, fwd), spec(GLA_V, fwd), spec(GLA_K, fwd),
                  spec(GLA_K, bwd), spec(GLA_K, bwd), spec(GLA_V, bwd),
                  spec(GLA_K, lambda i: (0, bwd_blk(i), 1))],
        out_specs=[spec(GLA_V, fwd), spec(GLA_V, bwd)],
        out_shape=[jax.ShapeDtypeStruct((bsz, seq, GLA_V), F32)] * 2,
        scratch_shapes=[pltpu.VMEM((2 * bsz, GLA_V, GLA_K), F32)],
        name="gla_scan",
        compiler_params=_cparams(("arbitrary",)),
    )(qg, kg, vg, la, qg, kg, vg, la)


def _attn_kernel(lq1_ref, lk1_ref, lq2_ref, lk2_ref, q_ref, k_ref, v_ref, o_ref,
                 m_ref, acc_ref, sa_ref, sb_ref, ma_ref, mb_ref, *, n_ctx, n_chunks, tk, lam_init):
    lam = (jnp.exp(jnp.sum(lq1_ref[...] * lk1_ref[...], axis=-1, keepdims=True))
           - jnp.exp(jnp.sum(lq2_ref[...] * lk2_ref[...], axis=-1, keepdims=True)) + lam_init)
    q = q_ref[0]
    lane = lax.broadcasted_iota(jnp.int32, (1, LANES), 1)
    zero = jnp.zeros_like(q)
    qs = (jnp.where(lane < DIFF_DH, q, zero), jnp.where(lane >= DIFF_DH, q, zero))
    m_ref[...] = jnp.full_like(m_ref, -jnp.inf)
    acc_ref[...] = jnp.zeros_like(acc_ref)

    def scores(start, size, s_ref, mx_ref):
        k = k_ref[0, pl.ds(start, size), :]
        for mp in range(2):
            s = _dot_nt(qs[mp], k)
            s_ref[mp, :, 0:size] = s
            mx_ref[mp] = jnp.max(s, axis=-1, keepdims=True)

    def consume(start, size, s_ref, mx_ref):
        v_ext = jnp.concatenate([v_ref[0, pl.ds(start, size), :], jnp.ones((size, LANES), BF16)], axis=1)
        for mp in range(2):
            m_old = m_ref[mp]
            m_new = jnp.maximum(m_old, mx_ref[mp])
            p = jnp.exp2(s_ref[mp, :, 0:size] - m_new).astype(BF16)
            acc_ref[mp] = jnp.exp2(m_old - m_new) * acc_ref[mp] + _dot(p, v_ext)
            m_ref[mp] = m_new

    def latent(j):
        return pl.multiple_of(n_ctx + j * tk, math.gcd(n_ctx, tk))

    buf_a, buf_b = (sa_ref, ma_ref), (sb_ref, mb_ref)
    scores(0, n_ctx, *buf_a)
    if n_chunks == 0:
        consume(0, n_ctx, *buf_a)
    else:
        scores(latent(0), tk, *buf_b)
        consume(0, n_ctx, *buf_a)

        def chunk_pair(jj, carry):
            scores(latent(2 * jj + 1), tk, *buf_a)
            consume(latent(2 * jj), tk, *buf_b)
            scores(latent(jnp.minimum(2 * jj + 2, n_chunks - 1)), tk, *buf_b)
            consume(latent(2 * jj + 1), tk, *buf_a)
            return carry

        lax.fori_loop(0, n_chunks // 2, chunk_pair, 0)
    o_ref[0] = (acc_ref[0, :, 0:LANES] / acc_ref[0, :, LANES:]
                - lam * (acc_ref[1, :, 0:LANES] / acc_ref[1, :, LANES:]))


def _attention(qd, kd, vd, lam_vecs, lam_init, n_q, n_ctx, n_kv, tq, tk):
    bsz = qd.shape[0]
    n_chunks = (n_kv - n_ctx) // tk
    assert n_chunks % 2 == 0 and n_ctx + n_chunks * tk == n_kv and n_q % tq == 0
    lam_spec = pl.BlockSpec((1, DIFF_DH), lambda b, h, i: (0, 0))
    return pl.pallas_call(
        functools.partial(_attn_kernel, n_ctx=n_ctx, n_chunks=n_chunks, tk=tk, lam_init=lam_init),
        grid=(bsz, DIFF_HEADS, n_q // tq),
        in_specs=[lam_spec] * 4 + [
            pl.BlockSpec((1, tq, LANES), lambda b, h, i: (b, i, h)),
            pl.BlockSpec((1, n_kv, LANES), lambda b, h, i: (b, 0, h)),
            pl.BlockSpec((1, n_kv, LANES), lambda b, h, i: (b, 0, h))],
        out_specs=pl.BlockSpec((1, tq, LANES), lambda b, h, i: (b, i, h)),
        out_shape=jax.ShapeDtypeStruct((bsz, n_q, DIFF_V), F32),
        scratch_shapes=[pltpu.VMEM((2, tq, 1), F32), pltpu.VMEM((2, tq, 2 * LANES), F32),
                        pltpu.VMEM((2, tq, max(tk, n_ctx)), F32), pltpu.VMEM((2, tq, max(tk, n_ctx)), F32),
                        pltpu.VMEM((2, tq, 1), F32), pltpu.VMEM((2, tq, 1), F32)],
        name="diff_attn",
        compiler_params=_cparams(("parallel", "parallel", "arbitrary")),
    )(*lam_vecs, qd, kd, vd)


def _postmix_kernel(x_ref, of_ref, ob_ref, og_ref, odc_ref, odl_ref, pc_ref, pp_ref, pn_ref,
                    gn_ref, dn_ref, pw_ref, ps_ref, wo_ref, g1_ref, sh2_ref, sc2_ref, n2_ref,
                    wr_ref, br_ref, x1_ref, h2_ref, route_ref, ext_ref,
                    *, tm, nct, n_ctx, seq, lam_init):
    i = pl.program_id(1)

    a = of_ref[0] + ob_ref[0]
    avg = ((lax.broadcasted_iota(jnp.int32, (GLA_V, GLA_V), 0) // GLA_DV
            == lax.broadcasted_iota(jnp.int32, (GLA_V, GLA_V), 1) // GLA_DV).astype(F32) * (1.0 / GLA_DV)).astype(BF16)
    sq_hi, sq_lo = _split_bf16(a * a)
    gla = a * lax.rsqrt(_dot(sq_hi, avg) + _dot(sq_lo, avg) + EPS) * gn_ref[...] * _silu(og_ref[0])
    y = _dot(gla.astype(BF16), wo_ref[0:GLA_V, :])

    for hd in range(DIFF_HEADS):
        lo = hd * DIFF_DV
        od = jnp.where(i < nct, odc_ref[0, :, lo:lo + DIFF_DV], odl_ref[0, :, lo:lo + DIFF_DV])
        dh = _rms(od) * dn_ref[:, lo:lo + DIFF_DV] * (1.0 - lam_init)
        y = y + _dot(dh.astype(BF16), wo_ref[GLA_V + lo:GLA_V + lo + DIFF_DV, :])

    seg_lo = jnp.where(i < nct, 0, n_ctx)
    seg_hi = jnp.where(i < nct, n_ctx, seq)
    ext_ref[0:POOL_HALO] = pp_ref[0]
    ext_ref[POOL_HALO:POOL_HALO + tm] = pc_ref[0]
    ext_ref[POOL_HALO + tm:] = pn_ref[0]
    pos_e = i * tm - POOL_HALO + lax.broadcasted_iota(jnp.int32, (tm + 2 * POOL_HALO, 1), 0)
    e = jnp.where((pos_e >= seg_lo) & (pos_e < seg_hi), ext_ref[...], 0.0)
    pos = i * tm + lax.broadcasted_iota(jnp.int32, (tm, 1), 0)
    grp = lax.broadcasted_iota(jnp.int32, (1, POOL_W), 1) // POOL_CH
    run, width, mean = e, 1, jnp.zeros((tm, POOL_W), F32)
    for gi, w in enumerate(POOL_WINDOWS):
        while width < w:
            n = run.shape[0] - width
            run = run[0:n] + run[width:width + n]
            width *= 2
        start = POOL_HALO - w // 2
        cnt = (jnp.minimum(pos + (w - w // 2), seg_hi) - jnp.maximum(pos - w // 2, seg_lo)).astype(F32)
        mean = jnp.where(grp == gi, run[start:start + tm] / cnt, mean)
    pooled = _dot((mean - pc_ref[0]).astype(BF16), pw_ref[...]) * ps_ref[...]
    y = y + _dot(pooled.astype(BF16), wo_ref[GLA_V + DIFF_V:, :])

    x1 = x_ref[0] + g1_ref[0] * y
    x1_ref[0] = x1
    h2 = _rms(x1) * n2_ref[...] * (1.0 + sc2_ref[0]) + sh2_ref[0]
    bits = lax.bitcast_convert_type(h2.astype(BF16).astype(F32), jnp.uint32)
    half = bits.shape[1] // 2
    h2_ref[0] = (bits[:, 0:half] >> 16) | (bits[:, half:] & jnp.uint32(0xFFFF0000))

    h_hi, h_lo = _split_bf16(h2)
    both = _dot(h_hi, wr_ref[...])
    logit = both[:, 0:ROUTE_W] + both[:, ROUTE_W:] + _dot(h_lo, wr_ref[:, 0:ROUTE_W]) + br_ref[...]
    lane = lax.broadcasted_iota(jnp.int32, (1, ROUTE_W), 1).astype(F32)
    neg = -jnp.inf

    def top(vals):
        mx = jnp.max(vals, axis=-1, keepdims=True)
        idx = jnp.min(jnp.where(vals == mx, lane, float(ROUTE_W)), axis=-1, keepdims=True)
        return mx, idx

    gl = jnp.where(lane < N_GROUPS, logit, neg)
    gmax, gidx = top(gl)
    g_top = 1.0 / jnp.sum(jnp.exp(gl - gmax), axis=-1, keepdims=True)
    e_lo = N_GROUPS + gidx * EXPERTS_PER_GROUP
    el = jnp.where((lane >= e_lo) & (lane < e_lo + EXPERTS_PER_GROUP), logit, neg)
    emax, idx1 = top(el)
    esum = jnp.sum(jnp.exp(el - emax), axis=-1, keepdims=True)
    emax2, idx2 = top(jnp.where(lane == idx1, neg, el))
    e1 = 1.0 / esum
    e2 = jnp.exp(emax2 - emax) / esum
    w1 = g_top * e1 / (e1 + e2)
    w2 = g_top * e2 / (e1 + e2)
    rec = jnp.where(lane == 0, idx1 - N_GROUPS, 0.0)
    rec = jnp.where(lane == 1, idx2 - N_GROUPS, rec)
    rec = jnp.where(lane == 2, w1, rec)
    route_ref[0] = jnp.where(lane == 3, w2, rec)


def _postmix(xall, o_f, o_b, og, od_ctx, od_lat, pool, mod, prm, nct, n_ctx, tm, lam_init):
    bsz, seq, d = xall.shape
    hpb = tm // POOL_HALO
    n_halo = seq // POOL_HALO

    def mod_spec(col):
        return pl.BlockSpec((1, 1, d), lambda b, i: (jnp.where(i < nct, bsz, b), 0, col))

    tile = lambda w: pl.BlockSpec((1, tm, w), lambda b, i: (b, i, 0))
    full = lambda r, c: pl.BlockSpec((r, c), lambda b, i: (0, 0))
    return pl.pallas_call(
        functools.partial(_postmix_kernel, tm=tm, nct=nct, n_ctx=n_ctx, seq=seq, lam_init=lam_init),
        grid=(bsz, seq // tm),
        in_specs=[tile(d), tile(GLA_V), tile(GLA_V), tile(GLA_V),
                  pl.BlockSpec((1, tm, DIFF_V), lambda b, i: (b, jnp.minimum(i, nct - 1), 0)),
                  pl.BlockSpec((1, tm, DIFF_V), lambda b, i: (b, jnp.maximum(i - nct, 0), 0)),
                  tile(POOL_W),
                  pl.BlockSpec((1, POOL_HALO, POOL_W), lambda b, i: (b, jnp.maximum(i * hpb - 1, 0), 0)),
                  pl.BlockSpec((1, POOL_HALO, POOL_W), lambda b, i: (b, jnp.minimum((i + 1) * hpb, n_halo - 1), 0)),
                  full(1, GLA_V), full(1, DIFF_V), full(POOL_W, POOL_W), full(1, POOL_W), full(d, d),
                  mod_spec(2), mod_spec(3), mod_spec(4), full(1, d), full(d, 2 * ROUTE_W), full(1, ROUTE_W)],
        out_specs=[tile(d), tile(d // 2), tile(ROUTE_W)],
        out_shape=[jax.ShapeDtypeStruct((bsz, seq, d), F32), jax.ShapeDtypeStruct((bsz, seq, d // 2), jnp.uint32),
                   jax.ShapeDtypeStruct((bsz, seq, ROUTE_W), F32)],
        scratch_shapes=[pltpu.VMEM((tm + 2 * POOL_HALO, POOL_W), F32)],
        name="postmix",
        compiler_params=_cparams(("parallel", "parallel")),
    )(xall, o_f, o_b, og, od_ctx, od_lat, pool, pool, pool, prm["gla_norm"], prm["diff_norm"], prm["pool_w"],
      prm["pool_scale"], prm["w_out"], mod, mod, mod, prm["norm2"], prm["w_route"], prm["b_route"])


def _slot_of_assignment(expert, n_tok):
    n_assign = n_tok * TOP_K
    e = expert.reshape(n_assign)
    hot = (e[:, None] == jnp.arange(N_EXPERTS, dtype=jnp.int32)[None, :]).astype(jnp.int32)
    csum = jnp.cumsum(hot, axis=0)
    counts = csum[-1]
    rank = jnp.sum(csum * hot, axis=1) - 1
    padded = (counts + MOE_BLOCK - 1) // MOE_BLOCK * MOE_BLOCK
    padded_end = jnp.cumsum(padded)
    dest = (padded_end - padded)[e] + rank
    n_slots = -(-n_assign // MOE_BLOCK) * MOE_BLOCK + N_EXPERTS * MOE_BLOCK
    block_start = jnp.arange(n_slots // MOE_BLOCK, dtype=jnp.int32) * MOE_BLOCK
    block_expert = jnp.minimum(jnp.sum((padded_end[None, :] <= block_start[:, None]).astype(jnp.int32), axis=1),
                               N_EXPERTS - 1)
    return dest, block_expert, n_slots


def _each(n, fn):
    def step(r, carry):
        fn(r)
        return carry
    lax.fori_loop(0, n, step, 0, unroll=8)


def _dispatch_kernel(dcur_ref, dprev_ref, h_ref, xs_in, xs_hbm, buf, sem, *, tile):
    del xs_in
    i = pl.program_id(0)
    n_tiles = pl.num_programs(0) - 1

    def copy(step, dest_ref, r, k):
        return pltpu.make_async_copy(buf.at[step & 1, pl.ds(r, 1), :],
                                     xs_hbm.at[pl.ds(dest_ref[0, 0, r * TOP_K + k], 1), :], sem.at[step & 1])

    @pl.when(i < n_tiles)
    def _():
        buf[i & 1] = h_ref[...]
        _each(tile, lambda r: [copy(i, dcur_ref, r, k).start() for k in range(TOP_K)])

    @pl.when(i > 0)
    def _():
        _each(tile, lambda r: [copy(i - 1, dprev_ref, r, k).wait() for k in range(TOP_K)])


def _dispatch(h2p, dest, n_slots, tile):
    n_tok, w = h2p.shape
    n_tiles = n_tok // tile
    dest3 = dest.reshape(n_tiles, 1, tile * TOP_K)
    smem = lambda imap: pl.BlockSpec((1, 1, tile * TOP_K), imap, memory_space=pltpu.SMEM)
    return pl.pallas_call(
        functools.partial(_dispatch_kernel, tile=tile),
        grid=(n_tiles + 1,),
        in_specs=[smem(lambda i: (jnp.minimum(i, n_tiles - 1), 0, 0)), smem(lambda i: (jnp.maximum(i - 1, 0), 0, 0)),
                  pl.BlockSpec((tile, w), lambda i: (jnp.minimum(i, n_tiles - 1), 0)),
                  pl.BlockSpec(memory_space=pl.ANY)],
        out_specs=pl.BlockSpec(memory_space=pl.ANY),
        out_shape=jax.ShapeDtypeStruct((n_slots, w), jnp.uint32),
        scratch_shapes=[pltpu.VMEM((2, tile, w), jnp.uint32), pltpu.SemaphoreType.DMA((2,))],
        input_output_aliases={3: 0},
        name="moe_dispatch",
        compiler_params=_cparams(("arbitrary",)),
    )(dest3, dest3, h2p, jnp.zeros((n_slots, w), jnp.uint32))


def _expert_kernel(be_ref, xs_ref, w1_ref, w3_ref, w2_ref, ys_ref):
    del be_ref
    bits = xs_ref[...]
    half = bits.shape[1]
    x_lo = lax.bitcast_convert_type(bits << 16, F32).astype(BF16)
    x_hi = lax.bitcast_convert_type(bits & jnp.uint32(0xFFFF0000), F32).astype(BF16)

    def up(w_ref):
        return _dot(x_lo, w_ref[0, 0:half, :]) + _dot(x_hi, w_ref[0, half:, :])

    ys_ref[...] = _dot((_silu(up(w1_ref)) * up(w3_ref)).astype(BF16), w2_ref[0])


def _experts(xs, block_expert, w1, w3, w2):
    n_slots, half = xs.shape
    _, d, d_exp = w1.shape
    return pl.pallas_call(
        _expert_kernel,
        grid_spec=pltpu.PrefetchScalarGridSpec(
            num_scalar_prefetch=1,
            grid=(n_slots // MOE_BLOCK,),
            in_specs=[pl.BlockSpec((MOE_BLOCK, half), lambda i, be: (i, 0)),
                      pl.BlockSpec((1, d, d_exp), lambda i, be: (be[i], 0, 0)),
                      pl.BlockSpec((1, d, d_exp), lambda i, be: (be[i], 0, 0)),
                      pl.BlockSpec((1, d_exp, d), lambda i, be: (be[i], 0, 0))],
            out_specs=pl.BlockSpec((MOE_BLOCK, d), lambda i, be: (i, 0))),
        out_shape=jax.ShapeDtypeStruct((n_slots, d), F32),
        name="moe_experts",
        compiler_params=_cparams(("arbitrary",)),
    )(block_expert, xs, w1, w3, w2)


def _combine_kernel(dcur_ref, dnext_ref, x1_ref, route_ref, g2_ref, fn_ref, ys_hbm, o_ref, ybuf, sem,
                    *, final, tile):
    i = pl.program_id(0)
    n = pl.num_programs(0)

    def fetch(step, dest_ref, r, k):
        slot = step & 1
        return pltpu.make_async_copy(ys_hbm.at[pl.ds(dest_ref[0, 0, r * TOP_K + k], 1), :],
                                     ybuf.at[slot, k, pl.ds(r, 1), :], sem.at[slot])

    @pl.when(i == 0)
    def _():
        _each(tile, lambda r: [fetch(i, dcur_ref, r, k).start() for k in range(TOP_K)])

    @pl.when(i + 1 < n)
    def _():
        _each(tile, lambda r: [fetch(i + 1, dnext_ref, r, k).start() for k in range(TOP_K)])

    _each(tile, lambda r: [fetch(i, dcur_ref, r, k).wait() for k in range(TOP_K)])
    route = route_ref[0]
    lane = lax.broadcasted_iota(jnp.int32, (1, ROUTE_W), 1)
    w0 = jnp.sum(jnp.where(lane == 2, route, 0.0), axis=-1, keepdims=True)
    w1 = jnp.sum(jnp.where(lane == 3, route, 0.0), axis=-1, keepdims=True)
    x = x1_ref[0] + g2_ref[0] * (ybuf[i & 1, 0] * w0 + ybuf[i & 1, 1] * w1)
    o_ref[0] = _rms(x) * fn_ref[...] if final else x


def _combine(x1, ys, dest, route, mod, final_norm, nct, tm, final):
    bsz, seq, d = x1.shape
    off = nct if final else 0
    tpb = seq // tm - off
    per_b = seq // tm

    def tok_tile(i):
        return (i // tpb) * per_b + off + i % tpb

    n_steps = bsz * tpb
    dest3 = dest.reshape(bsz * per_b, 1, tm * TOP_K)
    smem = lambda imap: pl.BlockSpec((1, 1, tm * TOP_K), imap, memory_space=pltpu.SMEM)
    tile = lambda w: pl.BlockSpec((1, tm, w), lambda i: (i // tpb, off + i % tpb, 0))
    return pl.pallas_call(
        functools.partial(_combine_kernel, final=final, tile=tm),
        grid=(n_steps,),
        in_specs=[smem(lambda i: (tok_tile(i), 0, 0)),
                  smem(lambda i: (tok_tile(jnp.minimum(i + 1, n_steps - 1)), 0, 0)),
                  tile(d), tile(ROUTE_W),
                  pl.BlockSpec((1, 1, d), lambda i: (jnp.where(off + i % tpb < nct, bsz, i // tpb), 0, 5)),
                  pl.BlockSpec((1, d), lambda i: (0, 0)),
                  pl.BlockSpec(memory_space=pl.ANY)],
        out_specs=pl.BlockSpec((1, tm, d), lambda i: (i // tpb, i % tpb, 0)),
        out_shape=jax.ShapeDtypeStruct((bsz, tpb * tm, d), F32),
        scratch_shapes=[pltpu.VMEM((2, TOP_K, tm, d), F32), pltpu.SemaphoreType.DMA((2,))],
        name="combine",
        compiler_params=_cparams(("arbitrary",)),
    )(dest3, dest3, x1, route, mod, final_norm, ys)


def _rope_tables(n_ctx, n_lat):
    t = jnp.arange(n_lat, dtype=jnp.int32)
    row = (t // GRID_W).astype(F32)
    col = (t % GRID_W).astype(F32)
    inv = 1.0 / (ROPE_BASE ** (jnp.arange(0, AX_DIM, 2, dtype=F32) / AX_DIM))
    lane = jnp.arange(LANES)
    within = lane % DIFF_DH
    pos = jnp.where((within < AX_DIM)[None, :], row[:, None], col[:, None])
    ang = pos * inv[within % (AX_DIM // 2)][None, :]
    sign = jnp.where((within % AX_DIM) < AX_DIM // 2, -1.0, 1.0)[None, :]
    cos = jnp.concatenate([jnp.ones((n_ctx, LANES), F32), jnp.cos(ang)], axis=0)
    sin = jnp.concatenate([jnp.zeros((n_ctx, LANES), F32), jnp.sin(ang) * sign], axis=0)
    return cos, sin


def _pack_layer(layer, w_in, w_out, wa2_f, ba_f, wa2_b, ba_b, pool_w, wg, bg, we, be):
    d = w_in.shape[1]
    wi = w_in[layer]
    o = 0
    parts = {}
    for name, size in (("qg", GLA_K), ("kg", GLA_K), ("vg", GLA_V), ("og", GLA_V), ("af", GATE_RANK),
                       ("ab", GATE_RANK), ("qd", DIFF_QK), ("kd", DIFF_QK), ("vd", DIFF_V), ("pl", POOL_W)):
        parts[name] = wi[:, o:o + size]
        o += size
    gate = jnp.concatenate([parts["af"], parts["ab"], jnp.zeros((d, LANES - 2 * GATE_RANK), F32)], axis=1)
    w_all = jnp.concatenate([parts[n] for n in ("qg", "kg", "vg", "og", "qd", "kd", "vd", "pl")] + [gate],
                            axis=1).astype(BF16)
    wa2 = jnp.zeros((LANES, 2 * GLA_K), F32)
    wa2 = wa2.at[0:GATE_RANK, 0:GLA_K].set(wa2_f[layer])
    wa2 = wa2.at[GATE_RANK:2 * GATE_RANK, GLA_K:].set(wa2_b[layer])
    ba = jnp.concatenate([ba_f[layer], ba_b[layer]])[None, :]
    pw = jnp.zeros((POOL_W, POOL_W), F32)
    for gi in range(len(POOL_WINDOWS)):
        pw = pw.at[gi * POOL_CH:(gi + 1) * POOL_CH, gi * POOL_CH:(gi + 1) * POOL_CH].set(pool_w[layer, gi])
    w_route = jnp.concatenate([wg[layer], we[layer], jnp.zeros((d, ROUTE_W - N_GROUPS - N_EXPERTS), F32)], axis=1)
    b_route = jnp.concatenate([bg[layer], be[layer], jnp.zeros((ROUTE_W - N_GROUPS - N_EXPERTS,), F32)])[None, :]
    head = lax.bitcast_convert_type(lax.bitcast_convert_type(w_route, jnp.uint32) & jnp.uint32(0xFFFF0000), F32)
    w_route = jnp.concatenate([head.astype(BF16), (w_route - head).astype(BF16)], axis=1)
    return dict(w_all=w_all, wa2=wa2, ba=ba, pool_w=pw.astype(BF16), w_out=w_out[layer].astype(BF16),
                w_route=w_route, b_route=b_route)


def kernel(x, c, ctx, c_ctx, w_mod, b_mod, norm1, norm2, w_in, w_out, gla_wa2_f, gla_ba_f, gla_wa2_b, gla_ba_b, gla_norm, lam_q1, lam_k1, lam_q2, lam_k2, diff_norm, pool_w, pool_scale, router_wg, router_bg, router_we, router_be, exp_w1, exp_w3, exp_w2, final_norm):
    bsz, n_lat, d = x.shape
    n_ctx = ctx.shape[1]
    depth = w_mod.shape[0]
    seq = n_ctx + n_lat
    tm = math.gcd(256, n_ctx)
    nct = n_ctx // tm
    tq = math.gcd(512, n_lat)
    tk = math.gcd(1024, n_lat // 2)
    gla_rows = math.gcd(64, n_ctx)
    assert bsz + 1 <= 8 and n_lat % tm == 0 and n_lat % GRID_W == 0

    cond = jnp.concatenate([c, c_ctx[None, :], jnp.zeros((8 - bsz - 1, d), F32)], axis=0)
    mod_all = _adaln(cond, w_mod, b_mod)
    cos_t, sin_t = _rope_tables(n_ctx, n_lat)
    xall = jnp.concatenate([ctx, x], axis=1)
    n_tok = bsz * seq

    for layer in range(depth):
        last = layer == depth - 1
        lam_init = 0.8 - 0.6 * math.exp(-0.3 * layer)
        prm = _pack_layer(layer, w_in, w_out, gla_wa2_f, gla_ba_f, gla_wa2_b, gla_ba_b, pool_w,
                          router_wg, router_bg, router_we, router_be)
        prm.update(gla_norm=gla_norm[layer][None, :], diff_norm=diff_norm[layer][None, :],
                   pool_scale=pool_scale[layer][None, :], norm2=norm2[layer][None, :])
        mod = mod_all[layer].reshape(8, 1, 6 * d)

        qg, kg, vg, og, la, qd, kd, vd, pool = _premix(
            xall, mod, norm1[layer][None, :], prm["w_all"], prm["wa2"], prm["ba"], cos_t, sin_t, nct, tm)
        o_f, o_b = _gla(qg, kg, vg, la, n_ctx, gla_rows)
        lam_vecs = [v[layer][None, :] for v in (lam_q1, lam_k1, lam_q2, lam_k2)]
        od_lat = _attention(qd[:, n_ctx:], kd, vd, lam_vecs, lam_init, n_lat, n_ctx, seq, tq, tk)
        od_ctx = _attention(qd, kd, vd, lam_vecs, lam_init, n_ctx, n_ctx, n_ctx, tm, tk)
        x1, h2p, route = _postmix(xall, o_f, o_b, og, od_ctx, od_lat, pool, mod, prm, nct, n_ctx, tm, lam_init)

        expert = route[..., 0:TOP_K].astype(jnp.int32).reshape(n_tok, TOP_K)
        dest, block_expert, n_slots = _slot_of_assignment(expert, n_tok)
        xs = _dispatch(h2p.reshape(n_tok, d // 2), dest, n_slots, tm)
        ys = _experts(xs, block_expert, exp_w1[layer].astype(BF16), exp_w3[layer].astype(BF16),
                      exp_w2[layer].astype(BF16))
        xall = _combine(x1, ys, dest, route, mod, final_norm[None, :], nct, tm, last)
    return xall
```

```python
import functools
import math

import jax
import jax.numpy as jnp
from jax import lax
from jax.experimental import pallas as pl
from jax.experimental.pallas import tpu as pltpu

F32 = jnp.float32
BF16 = jnp.bfloat16
HIGHEST = lax.Precision.HIGHEST

EPS = 1e-6
GRID_W = 64
GLA_HEADS, GLA_DK, GLA_DV = 4, 32, 64
GLA_K, GLA_V = GLA_HEADS * GLA_DK, GLA_HEADS * GLA_DV
GATE_RANK, GATE_TEMP = 16, 16.0
DIFF_HEADS, DIFF_DH = 4, 64
DIFF_DV = 2 * DIFF_DH
DIFF_QK = DIFF_HEADS * 2 * DIFF_DH
DIFF_V = DIFF_HEADS * DIFF_DV
LOG2_E = math.log2(math.e)
Q_SCALE = DIFF_DH ** -0.5 * LOG2_E
ROPE_BASE = 10000.0
AX_DIM = DIFF_DH // 2
POOL_WINDOWS = (2, 4, 8, 16)
POOL_CH = 64
POOL_W = len(POOL_WINDOWS) * POOL_CH
POOL_HALO = 8
N_GROUPS, EXPERTS_PER_GROUP = 4, 4
N_EXPERTS = N_GROUPS * EXPERTS_PER_GROUP
TOP_K = 2
MOE_BLOCK = 256

LANES = 128
SUBLANES = 8
SUB = 16
GLA_ROWS = 64
ROUTE_W = LANES
TOKEN_TILE = 256
QUERY_TILE = 512
KEY_CHUNK_TARGET = 1280
ADALN_COLS = 1536
DMA_UNROLL = 8
HI16 = 0xFFFF0000
VMEM_LIMIT = 56 * 1024 * 1024


def _cparams(sem):
    return pltpu.CompilerParams(dimension_semantics=sem, vmem_limit_bytes=VMEM_LIMIT)


def _dot(a, b):
    return jnp.dot(a, b, preferred_element_type=F32)


def _dot_hi(a, b):
    return jnp.dot(a, b, precision=HIGHEST, preferred_element_type=F32)


def _dot_nt(a, b):
    return lax.dot_general(a, b, (((1,), (1,)), ((), ())), preferred_element_type=F32)


def _dot_tn(a, b):
    return lax.dot_general(a, b, (((0,), (0,)), ((), ())), preferred_element_type=F32)


def _split_bf16(x):
    hi = x.astype(BF16)
    return hi, (x - hi.astype(F32)).astype(BF16)


def _silu(x):
    return x * jax.nn.sigmoid(x)


def _log_sigmoid(x):
    return jnp.minimum(x, 0.0) - jnp.log1p(jnp.exp(-jnp.abs(x)))


def _rms(x):
    return x * lax.rsqrt(jnp.mean(x * x, axis=-1, keepdims=True) + EPS)


def _adaln_kernel(c_ref, w_ref, b_ref, o_ref):
    o_ref[0] = _dot_hi(_silu(c_ref[...]), w_ref[0]) + b_ref[0]


def _adaln(cond, w_mod, b_mod):
    depth, d, six_d = w_mod.shape
    tn = math.gcd(ADALN_COLS, six_d)
    return pl.pallas_call(
        _adaln_kernel,
        grid=(depth, six_d // tn),
        in_specs=[pl.BlockSpec((SUBLANES, d), lambda l, j: (0, 0)),
                  pl.BlockSpec((1, d, tn), lambda l, j: (l, 0, j)),
                  pl.BlockSpec((1, 1, tn), lambda l, j: (l, 0, j))],
        out_specs=pl.BlockSpec((1, SUBLANES, tn), lambda l, j: (l, 0, j)),
        out_shape=jax.ShapeDtypeStruct((depth, SUBLANES, six_d), F32),
        name="adaln",
        compiler_params=_cparams(("arbitrary", "arbitrary")),
    )(cond, w_mod, b_mod.reshape(depth, 1, six_d))


_C_QG, _C_KG, _C_VG, _C_OG = 0, GLA_K, 2 * GLA_K, 2 * GLA_K + GLA_V
_C_QD = 2 * GLA_K + 2 * GLA_V
_C_KD = _C_QD + DIFF_QK
_C_VD = _C_KD + DIFF_QK
_C_PL = _C_VD + DIFF_V
_C_GT = _C_PL + POOL_W
_C_END = _C_GT + LANES


def _premix_kernel(x_ref, sh_ref, sc_ref, n1_ref, w_ref, wa2_ref, ba_ref, cos_ref, sin_ref,
                   qg_ref, kg_ref, vg_ref, og_ref, la_ref, qd_ref, kd_ref, vd_ref, pool_ref):
    x = x_ref[0]
    h = _rms(x) * n1_ref[...]
    hb = (h * (1.0 + sc_ref[0]) + sh_ref[0]).astype(BF16)

    def proj(lo, hi):
        return _dot(hb, w_ref[:, lo:hi])

    qg_ref[0] = proj(_C_QG, _C_KG) * (GLA_DK ** -0.5)
    kg_ref[0] = proj(_C_KG, _C_VG)
    vg_ref[0] = proj(_C_VG, _C_OG).astype(BF16)
    og_ref[0] = proj(_C_OG, _C_QD)
    vd_ref[0] = proj(_C_VD, _C_PL).astype(BF16)
    pool_ref[0] = proj(_C_PL, _C_GT)
    pre = _dot_hi(proj(_C_GT, _C_END), wa2_ref[...]) + ba_ref[...]
    la_ref[0] = _log_sigmoid(pre) / GATE_TEMP

    cos = cos_ref[...]
    sin = sin_ref[...]
    lane = lax.broadcasted_iota(jnp.int32, (1, LANES), 1)
    first_half = (lane % AX_DIM) < (AX_DIM // 2)

    def rope(a):
        partner = jnp.where(first_half, pltpu.roll(a, LANES - AX_DIM // 2, 1), pltpu.roll(a, AX_DIM // 2, 1))
        return a * cos + partner * sin

    for hd in range(DIFF_HEADS):
        lo = hd * LANES
        qd_ref[0, :, lo:lo + LANES] = (rope(proj(_C_QD + lo, _C_QD + lo + LANES)) * Q_SCALE).astype(BF16)
        kd_ref[0, :, lo:lo + LANES] = rope(proj(_C_KD + lo, _C_KD + lo + LANES)).astype(BF16)


def _premix(xall, mod, norm1, w_all, wa2, ba, cos_t, sin_t, nct, tm):
    bsz, seq, d = xall.shape

    def mod_row(b, i):
        return jnp.where(i < nct, bsz, b)

    tile = lambda w: pl.BlockSpec((1, tm, w), lambda b, i: (b, i, 0))
    outs = [(GLA_K, F32), (GLA_K, F32), (GLA_V, BF16), (GLA_V, F32), (2 * GLA_K, F32),
            (DIFF_QK, BF16), (DIFF_QK, BF16), (DIFF_V, BF16), (POOL_W, F32)]
    return pl.pallas_call(
        _premix_kernel,
        grid=(bsz, seq // tm),
        in_specs=[tile(d),
                  pl.BlockSpec((1, 1, d), lambda b, i: (mod_row(b, i), 0, 0)),
                  pl.BlockSpec((1, 1, d), lambda b, i: (mod_row(b, i), 0, 1)),
                  pl.BlockSpec((1, d), lambda b, i: (0, 0)),
                  pl.BlockSpec((d, _C_END), lambda b, i: (0, 0)),
                  pl.BlockSpec((LANES, 2 * GLA_K), lambda b, i: (0, 0)),
                  pl.BlockSpec((1, 2 * GLA_K), lambda b, i: (0, 0)),
                  pl.BlockSpec((tm, LANES), lambda b, i: (i, 0)),
                  pl.BlockSpec((tm, LANES), lambda b, i: (i, 0))],
        out_specs=[tile(w) for w, _ in outs],
        out_shape=[jax.ShapeDtypeStruct((bsz, seq, w), dt) for w, dt in outs],
        name="premix",
        compiler_params=_cparams(("parallel", "parallel")),
    )(xall, mod, mod, norm1, w_all, wa2, ba, cos_t, sin_t)


def _gla_kernel(qf_ref, kf_ref, vf_ref, gf_ref, qb_ref, kb_ref, vb_ref, gb_ref,
                of_ref, ob_ref, st_ref, *, bsz, rows):
    @pl.when(pl.program_id(0) == 0)
    def _():
        st_ref[...] = jnp.zeros_like(st_ref)

    n_sub = rows // SUB
    rr = lax.broadcasted_iota(jnp.int32, (rows, rows), 0)
    cc = lax.broadcasted_iota(jnp.int32, (rows, rows), 1)
    same_sub = rr // SUB == cc // SUB
    tri = ((same_sub & (cc <= rr)).astype(BF16), (same_sub & (cc >= rr)).astype(BF16))
    t_idx = lax.broadcasted_iota(jnp.int32, (SUB, 1), 0)
    same_head = (lax.broadcasted_iota(jnp.int32, (GLA_V, GLA_K), 0) // GLA_DV
                 == lax.broadcasted_iota(jnp.int32, (GLA_V, GLA_K), 1) // GLA_DK)
    expand = (lax.broadcasted_iota(jnp.int32, (GLA_K, GLA_V), 0) // GLA_DK
              == lax.broadcasted_iota(jnp.int32, (GLA_K, GLA_V), 1) // GLA_DV).astype(BF16)
    chains = [(b, 0, qf_ref, kf_ref, vf_ref, gf_ref, of_ref) for b in range(bsz)]
    chains += [(b, 1, qb_ref, kb_ref, vb_ref, gb_ref, ob_ref) for b in range(bsz)]

    for b, rev, q_ref, k_ref, v_ref, g_ref, o_ref in chains:
        g_hi, g_lo = _split_bf16(g_ref[b] * LOG2_E)
        bloc_all = _dot(tri[rev], g_hi) + _dot(tri[rev], g_lo)
        order = range(n_sub - 1, -1, -1) if rev else range(n_sub)
        parts = {}
        for j in order:
            rs = slice(j * SUB, (j + 1) * SUB)
            q = q_ref[b, rs, :]
            k = k_ref[b, rs, :]
            vb = v_ref[b, rs, :]
            v = vb.astype(F32)
            bloc = bloc_all[rs]
            bend = bloc[0:1] if rev else bloc[SUB - 1:SUB]
            qk_terms = []
            for s in range(SUB):
                valid = (t_idx <= s) if rev else (t_idx >= s)
                rel = jnp.where(valid, bloc - bloc[s:s + 1], -jnp.inf)
                qk_terms.append((q * k[s:s + 1] * jnp.exp2(rel)).astype(BF16))
            att = _dot(jnp.concatenate(qk_terms, axis=0), expand)
            o_diag = att[0:SUB] * v[0:1]
            for s in range(1, SUB):
                o_diag = o_diag + att[s * SUB:(s + 1) * SUB] * v[s:s + 1]
            upd = _dot_tn(vb, (k * jnp.exp2(bend - bloc)).astype(BF16))
            parts[j] = ((q * jnp.exp2(bloc)).astype(BF16), o_diag, jnp.exp2(bend), jnp.where(same_head, upd, 0.0))
        st = st_ref[2 * b + rev]
        for j in order:
            q_dec, o_diag, decay, upd = parts[j]
            o_ref[b, j * SUB:(j + 1) * SUB, :] = _dot_nt(q_dec, st.astype(BF16)) + o_diag
            st = st * decay + upd
        st_ref[2 * b + rev] = st


def _gla(qg, kg, vg, la, n_ctx, rows):
    bsz, seq, _ = qg.shape
    nc = n_ctx // rows
    nblk = seq // rows

    def fwd(i):
        return (0, i, 0)

    def bwd_blk(i):
        return jnp.where(i < nc, nc - 1 - i, nblk + nc - 1 - i)

    def bwd(i):
        return (0, bwd_blk(i), 0)

    def spec(w, imap):
        return pl.BlockSpec((bsz, rows, w), imap)

    return pl.pallas_call(
        functools.partial(_gla_kernel, bsz=bsz, rows=rows),
        grid=(nblk,),
        in_specs=[spec(GLA_K, fwd), spec(GLA_K, fwd), spec(GLA_V, fwd), spec(GLA_K, fwd),
                  spec(GLA_K, bwd), spec(GLA_K, bwd), spec(GLA_V, bwd),
                  spec(GLA_K, lambda i: (0, bwd_blk(i), 1))],
        out_specs=[spec(GLA_V, fwd), spec(GLA_V, bwd)],
        out_shape=[jax.ShapeDtypeStruct((bsz, seq, GLA_V), F32)] * 2,
        scratch_shapes=[pltpu.VMEM((2 * bsz, GLA_V, GLA_K), F32)],
        name="gla_scan",
        compiler_params=_cparams(("arbitrary",)),
    )(qg, kg, vg, la, qg, kg, vg, la)


def _attn_kernel(lq1_ref, lk1_ref, lq2_ref, lk2_ref, q_ref, k_ref, v_ref, o_ref,
                 m_ref, acc_ref, sa_ref, sb_ref, ma_ref, mb_ref, *, n_chunks, tk, lam_init):
    lam = (jnp.exp(jnp.sum(lq1_ref[...] * lk1_ref[...], axis=-1, keepdims=True))
           - jnp.exp(jnp.sum(lq2_ref[...] * lk2_ref[...], axis=-1, keepdims=True)) + lam_init)
    q = q_ref[0]
    lane = lax.broadcasted_iota(jnp.int32, (1, LANES), 1)
    zero = jnp.zeros_like(q)
    qs = (jnp.where(lane < DIFF_DH, q, zero), jnp.where(lane >= DIFF_DH, q, zero))
    m_ref[...] = jnp.full_like(m_ref, -jnp.inf)
    acc_ref[...] = jnp.zeros_like(acc_ref)

    def keys(j):
        return pl.ds(pl.multiple_of(j * tk, tk), tk)

    def scores(j, s_ref, mx_ref):
        k = k_ref[0, keys(j), :]
        for mp in range(2):
            s = _dot_nt(qs[mp], k)
            s_ref[mp] = s
            mx_ref[mp] = jnp.max(s, axis=-1, keepdims=True)

    def consume(j, s_ref, mx_ref):
        v_ext = jnp.concatenate([v_ref[0, keys(j), :], jnp.ones((tk, LANES), BF16)], axis=1)
        for mp in range(2):
            m_old = m_ref[mp]
            m_new = jnp.maximum(m_old, mx_ref[mp])
            p = jnp.exp2(s_ref[mp] - m_new).astype(BF16)
            acc_ref[mp] = jnp.exp2(m_old - m_new) * acc_ref[mp] + _dot(p, v_ext)
            m_ref[mp] = m_new

    buf_a, buf_b = (sa_ref, ma_ref), (sb_ref, mb_ref)
    scores(0, *buf_a)

    def chunk_pair(jj, carry):
        scores(2 * jj + 1, *buf_b)
        consume(2 * jj, *buf_a)
        scores(2 * jj + 2, *buf_a)
        consume(2 * jj + 1, *buf_b)
        return carry

    lax.fori_loop(0, n_chunks // 2, chunk_pair, 0)
    consume(n_chunks - 1, *buf_a)
    o_ref[0] = (acc_ref[0, :, 0:LANES] / acc_ref[0, :, LANES:]
                - lam * (acc_ref[1, :, 0:LANES] / acc_ref[1, :, LANES:]))


def _key_chunk(n_kv, target):
    sizes = [c for c in range(LANES, n_kv + 1, LANES) if n_kv % c == 0 and (n_kv // c) % 2 == 1]
    return max([c for c in sizes if c <= target] or sizes[:1])


def _attention(qd, kd, vd, lam_vecs, lam_init, n_q, n_kv, tq):
    bsz = qd.shape[0]
    tk = _key_chunk(n_kv, KEY_CHUNK_TARGET)
    n_chunks = n_kv // tk
    assert n_chunks % 2 == 1 and n_q % tq == 0
    lam_spec = pl.BlockSpec((1, DIFF_DH), lambda b, h, i: (0, 0))
    return pl.pallas_call(
        functools.partial(_attn_kernel, n_chunks=n_chunks, tk=tk, lam_init=lam_init),
        grid=(bsz, DIFF_HEADS, n_q // tq),
        in_specs=[lam_spec] * 4 + [
            pl.BlockSpec((1, tq, LANES), lambda b, h, i: (b, i, h)),
            pl.BlockSpec((1, n_kv, LANES), lambda b, h, i: (b, 0, h)),
            pl.BlockSpec((1, n_kv, LANES), lambda b, h, i: (b, 0, h))],
        out_specs=pl.BlockSpec((1, tq, LANES), lambda b, h, i: (b, i, h)),
        out_shape=jax.ShapeDtypeStruct((bsz, n_q, DIFF_V), F32),
        scratch_shapes=[pltpu.VMEM((2, tq, 1), F32), pltpu.VMEM((2, tq, 2 * LANES), F32),
                        pltpu.VMEM((2, tq, tk), F32), pltpu.VMEM((2, tq, tk), F32),
                        pltpu.VMEM((2, tq, 1), F32), pltpu.VMEM((2, tq, 1), F32)],
        name="diff_attn",
        compiler_params=_cparams(("parallel", "parallel", "arbitrary")),
    )(*lam_vecs, qd, kd, vd)


def _postmix_kernel(x_ref, of_ref, ob_ref, og_ref, odc_ref, odl_ref, pc_ref, pp_ref, pn_ref,
                    gn_ref, dn_ref, pw_ref, ps_ref, wo_ref, g1_ref, sh2_ref, sc2_ref, n2_ref,
                    wr_ref, br_ref, x1_ref, h2_ref, route_ref, ext_ref,
                    *, tm, nct, n_ctx, seq, lam_init):
    i = pl.program_id(1)

    a = of_ref[0] + ob_ref[0]
    avg = ((lax.broadcasted_iota(jnp.int32, (GLA_V, GLA_V), 0) // GLA_DV
            == lax.broadcasted_iota(jnp.int32, (GLA_V, GLA_V), 1) // GLA_DV).astype(F32) * (1.0 / GLA_DV)).astype(BF16)
    sq_hi, sq_lo = _split_bf16(a * a)
    gla = a * lax.rsqrt(_dot(sq_hi, avg) + _dot(sq_lo, avg) + EPS) * gn_ref[...] * _silu(og_ref[0])
    y = _dot(gla.astype(BF16), wo_ref[0:GLA_V, :])

    for hd in range(DIFF_HEADS):
        lo = hd * DIFF_DV
        od = jnp.where(i < nct, odc_ref[0, :, lo:lo + DIFF_DV], odl_ref[0, :, lo:lo + DIFF_DV])
        dh = _rms(od) * dn_ref[:, lo:lo + DIFF_DV] * (1.0 - lam_init)
        y = y + _dot(dh.astype(BF16), wo_ref[GLA_V + lo:GLA_V + lo + DIFF_DV, :])

    seg_lo = jnp.where(i < nct, 0, n_ctx)
    seg_hi = jnp.where(i < nct, n_ctx, seq)
    ext_ref[0:POOL_HALO] = pp_ref[0]
    ext_ref[POOL_HALO:POOL_HALO + tm] = pc_ref[0]
    ext_ref[POOL_HALO + tm:] = pn_ref[0]
    pos_e = i * tm - POOL_HALO + lax.broadcasted_iota(jnp.int32, (tm + 2 * POOL_HALO, 1), 0)
    e = jnp.where((pos_e >= seg_lo) & (pos_e < seg_hi), ext_ref[...], 0.0)
    pos = i * tm + lax.broadcasted_iota(jnp.int32, (tm, 1), 0)
    grp = lax.broadcasted_iota(jnp.int32, (1, POOL_W), 1) // POOL_CH
    run, width, mean = e, 1, jnp.zeros((tm, POOL_W), F32)
    for gi, w in enumerate(POOL_WINDOWS):
        while width < w:
            n = run.shape[0] - width
            run = run[0:n] + run[width:width + n]
            width *= 2
        start = POOL_HALO - w // 2
        cnt = (jnp.minimum(pos + (w - w // 2), seg_hi) - jnp.maximum(pos - w // 2, seg_lo)).astype(F32)
        mean = jnp.where(grp == gi, run[start:start + tm] / cnt, mean)
    pooled = _dot((mean - pc_ref[0]).astype(BF16), pw_ref[...]) * ps_ref[...]
    y = y + _dot(pooled.astype(BF16), wo_ref[GLA_V + DIFF_V:, :])

    x1 = x_ref[0] + g1_ref[0] * y
    x1_ref[0] = x1
    h2 = _rms(x1) * n2_ref[...] * (1.0 + sc2_ref[0]) + sh2_ref[0]
    bits = lax.bitcast_convert_type(h2.astype(BF16).astype(F32), jnp.uint32)
    half = bits.shape[1] // 2
    h2_ref[0] = (bits[:, 0:half] >> 16) | (bits[:, half:] & jnp.uint32(HI16))

    h_hi, h_lo = _split_bf16(h2)
    both = _dot(h_hi, wr_ref[...])
    logit = both[:, 0:ROUTE_W] + both[:, ROUTE_W:] + _dot(h_lo, wr_ref[:, 0:ROUTE_W]) + br_ref[...]
    lane = lax.broadcasted_iota(jnp.int32, (1, ROUTE_W), 1).astype(F32)
    neg = -jnp.inf

    def top(vals):
        mx = jnp.max(vals, axis=-1, keepdims=True)
        idx = jnp.min(jnp.where(vals == mx, lane, float(ROUTE_W)), axis=-1, keepdims=True)
        return mx, idx

    gl = jnp.where(lane < N_GROUPS, logit, neg)
    gmax, gidx = top(gl)
    g_top = 1.0 / jnp.sum(jnp.exp(gl - gmax), axis=-1, keepdims=True)
    e_lo = N_GROUPS + gidx * EXPERTS_PER_GROUP
    el = jnp.where((lane >= e_lo) & (lane < e_lo + EXPERTS_PER_GROUP), logit, neg)
    emax, idx1 = top(el)
    esum = jnp.sum(jnp.exp(el - emax), axis=-1, keepdims=True)
    emax2, idx2 = top(jnp.where(lane == idx1, neg, el))
    e1 = 1.0 / esum
    e2 = jnp.exp(emax2 - emax) / esum
    w1 = g_top * e1 / (e1 + e2)
    w2 = g_top * e2 / (e1 + e2)
    rec = jnp.where(lane == 0, idx1 - N_GROUPS, 0.0)
    rec = jnp.where(lane == 1, idx2 - N_GROUPS, rec)
    rec = jnp.where(lane == 2, w1, rec)
    route_ref[0] = jnp.where(lane == 3, w2, rec)


def _postmix(xall, o_f, o_b, og, od_ctx, od_lat, pool, mod, prm, nct, n_ctx, tm, lam_init):
    bsz, seq, d = xall.shape
    hpb = tm // POOL_HALO
    n_halo = seq // POOL_HALO

    def mod_spec(col):
        return pl.BlockSpec((1, 1, d), lambda b, i: (jnp.where(i < nct, bsz, b), 0, col))

    tile = lambda w: pl.BlockSpec((1, tm, w), lambda b, i: (b, i, 0))
    full = lambda r, c: pl.BlockSpec((r, c), lambda b, i: (0, 0))
    return pl.pallas_call(
        functools.partial(_postmix_kernel, tm=tm, nct=nct, n_ctx=n_ctx, seq=seq, lam_init=lam_init),
        grid=(bsz, seq // tm),
        in_specs=[tile(d), tile(GLA_V), tile(GLA_V), tile(GLA_V),
                  pl.BlockSpec((1, tm, DIFF_V), lambda b, i: (b, jnp.minimum(i, nct - 1), 0)),
                  pl.BlockSpec((1, tm, DIFF_V), lambda b, i: (b, jnp.maximum(i - nct, 0), 0)),
                  tile(POOL_W),
                  pl.BlockSpec((1, POOL_HALO, POOL_W), lambda b, i: (b, jnp.maximum(i * hpb - 1, 0), 0)),
                  pl.BlockSpec((1, POOL_HALO, POOL_W), lambda b, i: (b, jnp.minimum((i + 1) * hpb, n_halo - 1), 0)),
                  full(1, GLA_V), full(1, DIFF_V), full(POOL_W, POOL_W), full(1, POOL_W), full(d, d),
                  mod_spec(2), mod_spec(3), mod_spec(4), full(1, d), full(d, 2 * ROUTE_W), full(1, ROUTE_W)],
        out_specs=[tile(d), tile(d // 2), tile(ROUTE_W)],
        out_shape=[jax.ShapeDtypeStruct((bsz, seq, d), F32), jax.ShapeDtypeStruct((bsz, seq, d // 2), jnp.uint32),
                   jax.ShapeDtypeStruct((bsz, seq, ROUTE_W), F32)],
        scratch_shapes=[pltpu.VMEM((tm + 2 * POOL_HALO, POOL_W), F32)],
        name="postmix",
        compiler_params=_cparams(("parallel", "parallel")),
    )(xall, o_f, o_b, og, od_ctx, od_lat, pool, pool, pool, prm["gla_norm"], prm["diff_norm"], prm["pool_w"],
      prm["pool_scale"], prm["w_out"], mod, mod, mod, prm["norm2"], prm["w_route"], prm["b_route"])


def _slot_of_assignment(expert, n_tok):
    n_assign = n_tok * TOP_K
    e = expert.reshape(n_assign)
    hot = (e[:, None] == jnp.arange(N_EXPERTS, dtype=jnp.int32)[None, :]).astype(jnp.int32)
    csum = jnp.cumsum(hot, axis=0)
    counts = csum[-1]
    rank = jnp.sum(csum * hot, axis=1) - 1
    padded = (counts + MOE_BLOCK - 1) // MOE_BLOCK * MOE_BLOCK
    padded_end = jnp.cumsum(padded)
    dest = (padded_end - padded)[e] + rank
    n_slots = -(-n_assign // MOE_BLOCK) * MOE_BLOCK + N_EXPERTS * MOE_BLOCK
    block_start = jnp.arange(n_slots // MOE_BLOCK, dtype=jnp.int32) * MOE_BLOCK
    block_expert = jnp.minimum(jnp.sum((padded_end[None, :] <= block_start[:, None]).astype(jnp.int32), axis=1),
                               N_EXPERTS - 1)
    return dest, block_expert, n_slots


def _each(n, fn):
    def step(r, carry):
        fn(r)
        return carry
    lax.fori_loop(0, n, step, 0, unroll=DMA_UNROLL)


def _dispatch_kernel(dcur_ref, dprev_ref, h_ref, xs_in, xs_hbm, buf, sem, *, tile):
    del xs_in
    i = pl.program_id(0)
    n_tiles = pl.num_programs(0) - 1

    def copy(step, dest_ref, r, k):
        return pltpu.make_async_copy(buf.at[step & 1, pl.ds(r, 1), :],
                                     xs_hbm.at[pl.ds(dest_ref[0, 0, r * TOP_K + k], 1), :], sem.at[step & 1])

    @pl.when(i < n_tiles)
    def _():
        buf[i & 1] = h_ref[...]
        _each(tile, lambda r: [copy(i, dcur_ref, r, k).start() for k in range(TOP_K)])

    @pl.when(i > 0)
    def _():
        _each(tile, lambda r: [copy(i - 1, dprev_ref, r, k).wait() for k in range(TOP_K)])


def _dispatch(h2p, dest, n_slots, tile):
    n_tok, w = h2p.shape
    n_tiles = n_tok // tile
    dest3 = dest.reshape(n_tiles, 1, tile * TOP_K)
    smem = lambda imap: pl.BlockSpec((1, 1, tile * TOP_K), imap, memory_space=pltpu.SMEM)
    return pl.pallas_call(
        functools.partial(_dispatch_kernel, tile=tile),
        grid=(n_tiles + 1,),
        in_specs=[smem(lambda i: (jnp.minimum(i, n_tiles - 1), 0, 0)), smem(lambda i: (jnp.maximum(i - 1, 0), 0, 0)),
                  pl.BlockSpec((tile, w), lambda i: (jnp.minimum(i, n_tiles - 1), 0)),
                  pl.BlockSpec(memory_space=pl.ANY)],
        out_specs=pl.BlockSpec(memory_space=pl.ANY),
        out_shape=jax.ShapeDtypeStruct((n_slots, w), jnp.uint32),
        scratch_shapes=[pltpu.VMEM((2, tile, w), jnp.uint32), pltpu.SemaphoreType.DMA((2,))],
        input_output_aliases={3: 0},
        name="moe_dispatch",
        compiler_params=_cparams(("arbitrary",)),
    )(dest3, dest3, h2p, jnp.zeros((n_slots, w), jnp.uint32))


def _expert_kernel(be_ref, xs_ref, w1_ref, w3_ref, w2_ref, ys_ref):
    del be_ref
    bits = xs_ref[...]
    half = bits.shape[1]
    x_lo = lax.bitcast_convert_type(bits << 16, F32).astype(BF16)
    x_hi = lax.bitcast_convert_type(bits & jnp.uint32(HI16), F32).astype(BF16)

    def up(w_ref):
        return _dot(x_lo, w_ref[0, 0:half, :]) + _dot(x_hi, w_ref[0, half:, :])

    ys_ref[...] = _dot((_silu(up(w1_ref)) * up(w3_ref)).astype(BF16), w2_ref[0])


def _experts(xs, block_expert, w1, w3, w2):
    n_slots, half = xs.shape
    _, d, d_exp = w1.shape
    return pl.pallas_call(
        _expert_kernel,
        grid_spec=pltpu.PrefetchScalarGridSpec(
            num_scalar_prefetch=1,
            grid=(n_slots // MOE_BLOCK,),
            in_specs=[pl.BlockSpec((MOE_BLOCK, half), lambda i, be: (i, 0)),
                      pl.BlockSpec((1, d, d_exp), lambda i, be: (be[i], 0, 0)),
                      pl.BlockSpec((1, d, d_exp), lambda i, be: (be[i], 0, 0)),
                      pl.BlockSpec((1, d_exp, d), lambda i, be: (be[i], 0, 0))],
            out_specs=pl.BlockSpec((MOE_BLOCK, d), lambda i, be: (i, 0))),
        out_shape=jax.ShapeDtypeStruct((n_slots, d), F32),
        name="moe_experts",
        compiler_params=_cparams(("arbitrary",)),
    )(block_expert, xs, w1, w3, w2)


def _combine_kernel(dcur_ref, dnext_ref, x1_ref, route_ref, g2_ref, fn_ref, ys_hbm, o_ref, ybuf, sem,
                    *, final, tile):
    i = pl.program_id(0)
    n = pl.num_programs(0)

    def fetch(step, dest_ref, r, k):
        slot = step & 1
        return pltpu.make_async_copy(ys_hbm.at[pl.ds(dest_ref[0, 0, r * TOP_K + k], 1), :],
                                     ybuf.at[slot, k, pl.ds(r, 1), :], sem.at[slot])

    @pl.when(i == 0)
    def _():
        _each(tile, lambda r: [fetch(i, dcur_ref, r, k).start() for k in range(TOP_K)])

    @pl.when(i + 1 < n)
    def _():
        _each(tile, lambda r: [fetch(i + 1, dnext_ref, r, k).start() for k in range(TOP_K)])

    _each(tile, lambda r: [fetch(i, dcur_ref, r, k).wait() for k in range(TOP_K)])
    route = route_ref[0]
    lane = lax.broadcasted_iota(jnp.int32, (1, ROUTE_W), 1)
    w0 = jnp.sum(jnp.where(lane == 2, route, 0.0), axis=-1, keepdims=True)
    w1 = jnp.sum(jnp.where(lane == 3, route, 0.0), axis=-1, keepdims=True)
    x = x1_ref[0] + g2_ref[0] * (ybuf[i & 1, 0] * w0 + ybuf[i & 1, 1] * w1)
    o_ref[0] = _rms(x) * fn_ref[...] if final else x


def _combine(x1, ys, dest, route, mod, final_norm, nct, tm, final):
    bsz, seq, d = x1.shape
    off = nct if final else 0
    tpb = seq // tm - off
    per_b = seq // tm

    def tok_tile(i):
        return (i // tpb) * per_b + off + i % tpb

    n_steps = bsz * tpb
    dest3 = dest.reshape(bsz * per_b, 1, tm * TOP_K)
    smem = lambda imap: pl.BlockSpec((1, 1, tm * TOP_K), imap, memory_space=pltpu.SMEM)
    tile = lambda w: pl.BlockSpec((1, tm, w), lambda i: (i // tpb, off + i % tpb, 0))
    return pl.pallas_call(
        functools.partial(_combine_kernel, final=final, tile=tm),
        grid=(n_steps,),
        in_specs=[smem(lambda i: (tok_tile(i), 0, 0)),
                  smem(lambda i: (tok_tile(jnp.minimum(i + 1, n_steps - 1)), 0, 0)),
                  tile(d), tile(ROUTE_W),
                  pl.BlockSpec((1, 1, d), lambda i: (jnp.where(off + i % tpb < nct, bsz, i // tpb), 0, 5)),
                  pl.BlockSpec((1, d), lambda i: (0, 0)),
                  pl.BlockSpec(memory_space=pl.ANY)],
        out_specs=pl.BlockSpec((1, tm, d), lambda i: (i // tpb, i % tpb, 0)),
        out_shape=jax.ShapeDtypeStruct((bsz, tpb * tm, d), F32),
        scratch_shapes=[pltpu.VMEM((2, TOP_K, tm, d), F32), pltpu.SemaphoreType.DMA((2,))],
        name="combine",
        compiler_params=_cparams(("arbitrary",)),
    )(dest3, dest3, x1, route, mod, final_norm, ys)


def _rope_tables(n_ctx, n_lat):
    t = jnp.arange(n_lat, dtype=jnp.int32)
    row = (t // GRID_W).astype(F32)
    col = (t % GRID_W).astype(F32)
    inv = 1.0 / (ROPE_BASE ** (jnp.arange(0, AX_DIM, 2, dtype=F32) / AX_DIM))
    lane = jnp.arange(LANES)
    within = lane % DIFF_DH
    pos = jnp.where((within < AX_DIM)[None, :], row[:, None], col[:, None])
    ang = pos * inv[within % (AX_DIM // 2)][None, :]
    sign = jnp.where((within % AX_DIM) < AX_DIM // 2, -1.0, 1.0)[None, :]
    cos = jnp.concatenate([jnp.ones((n_ctx, LANES), F32), jnp.cos(ang)], axis=0)
    sin = jnp.concatenate([jnp.zeros((n_ctx, LANES), F32), jnp.sin(ang) * sign], axis=0)
    return cos, sin


def _pack_layer(layer, w_in, w_out, wa2_f, ba_f, wa2_b, ba_b, pool_w, wg, bg, we, be):
    d = w_in.shape[1]
    wi = w_in[layer]
    o = 0
    parts = {}
    for name, size in (("qg", GLA_K), ("kg", GLA_K), ("vg", GLA_V), ("og", GLA_V), ("af", GATE_RANK),
                       ("ab", GATE_RANK), ("qd", DIFF_QK), ("kd", DIFF_QK), ("vd", DIFF_V), ("pl", POOL_W)):
        parts[name] = wi[:, o:o + size]
        o += size
    gate = jnp.concatenate([parts["af"], parts["ab"], jnp.zeros((d, LANES - 2 * GATE_RANK), F32)], axis=1)
    w_all = jnp.concatenate([parts[n] for n in ("qg", "kg", "vg", "og", "qd", "kd", "vd", "pl")] + [gate],
                            axis=1).astype(BF16)
    wa2 = jnp.zeros((LANES, 2 * GLA_K), F32)
    wa2 = wa2.at[0:GATE_RANK, 0:GLA_K].set(wa2_f[layer])
    wa2 = wa2.at[GATE_RANK:2 * GATE_RANK, GLA_K:].set(wa2_b[layer])
    ba = jnp.concatenate([ba_f[layer], ba_b[layer]])[None, :]
    pw = jnp.zeros((POOL_W, POOL_W), F32)
    for gi in range(len(POOL_WINDOWS)):
        pw = pw.at[gi * POOL_CH:(gi + 1) * POOL_CH, gi * POOL_CH:(gi + 1) * POOL_CH].set(pool_w[layer, gi])
    w_route = jnp.concatenate([wg[layer], we[layer], jnp.zeros((d, ROUTE_W - N_GROUPS - N_EXPERTS), F32)], axis=1)
    b_route = jnp.concatenate([bg[layer], be[layer], jnp.zeros((ROUTE_W - N_GROUPS - N_EXPERTS,), F32)])[None, :]
    head = lax.bitcast_convert_type(lax.bitcast_convert_type(w_route, jnp.uint32) & jnp.uint32(HI16), F32)
    w_route = jnp.concatenate([head.astype(BF16), (w_route - head).astype(BF16)], axis=1)
    return dict(w_all=w_all, wa2=wa2, ba=ba, pool_w=pw.astype(BF16), w_out=w_out[layer].astype(BF16),
                w_route=w_route, b_route=b_route)


def kernel(x, c, ctx, c_ctx, w_mod, b_mod, norm1, norm2, w_in, w_out, gla_wa2_f, gla_ba_f, gla_wa2_b, gla_ba_b, gla_norm, lam_q1, lam_k1, lam_q2, lam_k2, diff_norm, pool_w, pool_scale, router_wg, router_bg, router_we, router_be, exp_w1, exp_w3, exp_w2, final_norm):
    bsz, n_lat, d = x.shape
    n_ctx = ctx.shape[1]
    depth = w_mod.shape[0]
    seq = n_ctx + n_lat
    tm = math.gcd(TOKEN_TILE, n_ctx)
    nct = n_ctx // tm
    tq = math.gcd(QUERY_TILE, n_lat)
    gla_rows = math.gcd(GLA_ROWS, n_ctx)
    assert bsz + 1 <= SUBLANES and n_lat % tm == 0 and n_lat % GRID_W == 0

    cond = jnp.concatenate([c, c_ctx[None, :], jnp.zeros((SUBLANES - bsz - 1, d), F32)], axis=0)
    mod_all = _adaln(cond, w_mod, b_mod)
    cos_t, sin_t = _rope_tables(n_ctx, n_lat)
    xall = jnp.concatenate([ctx, x], axis=1)
    n_tok = bsz * seq

    for layer in range(depth):
        last = layer == depth - 1
        lam_init = 0.8 - 0.6 * math.exp(-0.3 * layer)
        prm = _pack_layer(layer, w_in, w_out, gla_wa2_f, gla_ba_f, gla_wa2_b, gla_ba_b, pool_w,
                          router_wg, router_bg, router_we, router_be)
        prm.update(gla_norm=gla_norm[layer][None, :], diff_norm=diff_norm[layer][None, :],
                   pool_scale=pool_scale[layer][None, :], norm2=norm2[layer][None, :])
        mod = mod_all[layer].reshape(SUBLANES, 1, 6 * d)

        qg, kg, vg, og, la, qd, kd, vd, pool = _premix(
            xall, mod, norm1[layer][None, :], prm["w_all"], prm["wa2"], prm["ba"], cos_t, sin_t, nct, tm)
        o_f, o_b = _gla(qg, kg, vg, la, n_ctx, gla_rows)
        lam_vecs = [v[layer][None, :] for v in (lam_q1, lam_k1, lam_q2, lam_k2)]
        od_lat = _attention(qd[:, n_ctx:], kd, vd, lam_vecs, lam_init, n_lat, seq, tq)
        od_ctx = _attention(qd, kd, vd, lam_vecs, lam_init, n_ctx, n_ctx, tm)
        x1, h2p, route = _postmix(xall, o_f, o_b, og, od_ctx, od_lat, pool, mod, prm, nct, n_ctx, tm, lam_init)

        expert = route[..., 0:TOP_K].astype(jnp.int32).reshape(n_tok, TOP_K)
        dest, block_expert, n_slots = _slot_of_assignment(expert, n_tok)
        xs = _dispatch(h2p.reshape(n_tok, d // 2), dest, n_slots, tm)
        ys = _experts(xs, block_expert, exp_w1[layer].astype(BF16), exp_w3[layer].astype(BF16),
                      exp_w2[layer].astype(BF16))
        xall = _combine(x1, ys, dest, route, mod, final_norm[None, :], nct, tm, last)
    return xall
```

```python
import functools
import math

import jax
import jax.numpy as jnp
from jax import lax
from jax.experimental import pallas as pl
from jax.experimental.pallas import tpu as pltpu

F32 = jnp.float32
BF16 = jnp.bfloat16
HIGHEST = lax.Precision.HIGHEST

EPS = 1e-6
GRID_W = 64
GLA_HEADS, GLA_DK, GLA_DV = 4, 32, 64
GLA_K, GLA_V = GLA_HEADS * GLA_DK, GLA_HEADS * GLA_DV
GATE_RANK, GATE_TEMP = 16, 16.0
DIFF_HEADS, DIFF_DH = 4, 64
DIFF_DV = 2 * DIFF_DH
DIFF_QK = DIFF_HEADS * 2 * DIFF_DH
DIFF_V = DIFF_HEADS * DIFF_DV
LOG2_E = math.log2(math.e)
Q_SCALE = DIFF_DH ** -0.5 * LOG2_E
ROPE_BASE = 10000.0
AX_DIM = DIFF_DH // 2
POOL_WINDOWS = (2, 4, 8, 16)
POOL_CH = 64
POOL_W = len(POOL_WINDOWS) * POOL_CH
POOL_HALO = max(POOL_WINDOWS) // 2
N_GROUPS, EXPERTS_PER_GROUP = 4, 4
N_EXPERTS = N_GROUPS * EXPERTS_PER_GROUP
TOP_K = 2
MOE_BLOCK = 256

LANES = 128
SUBLANES = 8
SUB = 16
GLA_ROWS = 128
ROUTE_W = LANES
TOKEN_TILE = 256
QUERY_TILE = 512
KEY_CHUNK_TARGET = 1280
ADALN_COLS = 1536
DMA_UNROLL = 8
HI16 = 0xFFFF0000
VMEM_LIMIT = 56 * 1024 * 1024


def _cparams(sem):
    return pltpu.CompilerParams(dimension_semantics=sem, vmem_limit_bytes=VMEM_LIMIT)


def _dot(a, b):
    return jnp.dot(a, b, preferred_element_type=F32)


def _dot_hi(a, b):
    return jnp.dot(a, b, precision=HIGHEST, preferred_element_type=F32)


def _dot_nt(a, b):
    return lax.dot_general(a, b, (((1,), (1,)), ((), ())), preferred_element_type=F32)


def _dot_tn(a, b):
    return lax.dot_general(a, b, (((0,), (0,)), ((), ())), preferred_element_type=F32)


def _split_bf16(x):
    hi = x.astype(BF16)
    return hi, (x - hi.astype(F32)).astype(BF16)


def _silu(x):
    return x * jax.nn.sigmoid(x)


def _log_sigmoid(x):
    return jnp.minimum(x, 0.0) - jnp.log1p(jnp.exp(-jnp.abs(x)))


def _rms(x):
    return x * lax.rsqrt(jnp.mean(x * x, axis=-1, keepdims=True) + EPS)


def _adaln_kernel(c_ref, w_ref, b_ref, o_ref):
    o_ref[0] = _dot_hi(_silu(c_ref[...]), w_ref[0]) + b_ref[0]


def _adaln(cond, w_mod, b_mod):
    depth, d, six_d = w_mod.shape
    tn = math.gcd(ADALN_COLS, six_d)
    return pl.pallas_call(
        _adaln_kernel,
        grid=(depth, six_d // tn),
        in_specs=[pl.BlockSpec((SUBLANES, d), lambda l, j: (0, 0)),
                  pl.BlockSpec((1, d, tn), lambda l, j: (l, 0, j)),
                  pl.BlockSpec((1, 1, tn), lambda l, j: (l, 0, j))],
        out_specs=pl.BlockSpec((1, SUBLANES, tn), lambda l, j: (l, 0, j)),
        out_shape=jax.ShapeDtypeStruct((depth, SUBLANES, six_d), F32),
        name="adaln",
        compiler_params=_cparams(("arbitrary", "arbitrary")),
    )(cond, w_mod, b_mod.reshape(depth, 1, six_d))


_C_QG, _C_KG, _C_VG, _C_OG = 0, GLA_K, 2 * GLA_K, 2 * GLA_K + GLA_V
_C_QD = 2 * GLA_K + 2 * GLA_V
_C_KD = _C_QD + DIFF_QK
_C_VD = _C_KD + DIFF_QK
_C_PL = _C_VD + DIFF_V
_C_GT = _C_PL + POOL_W
_C_END = _C_GT + LANES


def _premix_kernel(x_ref, sh_ref, sc_ref, n1_ref, w_ref, wa2_ref, ba_ref, cos_ref, sin_ref,
                   qg_ref, kg_ref, vg_ref, og_ref, la_ref, qd_ref, kd_ref, vd_ref, pool_ref):
    x = x_ref[0]
    h = _rms(x) * n1_ref[...]
    hb = (h * (1.0 + sc_ref[0]) + sh_ref[0]).astype(BF16)

    def proj(lo, hi):
        return _dot(hb, w_ref[:, lo:hi])

    qg_ref[0] = proj(_C_QG, _C_KG) * (GLA_DK ** -0.5)
    kg_ref[0] = proj(_C_KG, _C_VG)
    vg_ref[0] = proj(_C_VG, _C_OG).astype(BF16)
    og_ref[0] = proj(_C_OG, _C_QD)
    vd_ref[0] = proj(_C_VD, _C_PL).astype(BF16)
    pool_ref[0] = proj(_C_PL, _C_GT)
    pre = _dot_hi(proj(_C_GT, _C_END), wa2_ref[...]) + ba_ref[...]
    la_ref[0] = _log_sigmoid(pre) / GATE_TEMP

    cos = cos_ref[...]
    sin = sin_ref[...]
    lane = lax.broadcasted_iota(jnp.int32, (1, LANES), 1)
    first_half = (lane % AX_DIM) < (AX_DIM // 2)

    def rope(a):
        partner = jnp.where(first_half, pltpu.roll(a, LANES - AX_DIM // 2, 1), pltpu.roll(a, AX_DIM // 2, 1))
        return a * cos + partner * sin

    for hd in range(DIFF_HEADS):
        lo = hd * LANES
        qd_ref[0, :, lo:lo + LANES] = (rope(proj(_C_QD + lo, _C_QD + lo + LANES)) * Q_SCALE).astype(BF16)
        kd_ref[0, :, lo:lo + LANES] = rope(proj(_C_KD + lo, _C_KD + lo + LANES)).astype(BF16)


def _premix(xall, mod, norm1, w_all, wa2, ba, cos_t, sin_t, nct, tm):
    bsz, seq, d = xall.shape

    def mod_row(b, i):
        return jnp.where(i < nct, bsz, b)

    tile = lambda w: pl.BlockSpec((1, tm, w), lambda b, i: (b, i, 0))
    outs = [(GLA_K, F32), (GLA_K, F32), (GLA_V, BF16), (GLA_V, F32), (2 * GLA_K, F32),
            (DIFF_QK, BF16), (DIFF_QK, BF16), (DIFF_V, BF16), (POOL_W, F32)]
    return pl.pallas_call(
        _premix_kernel,
        grid=(bsz, seq // tm),
        in_specs=[tile(d),
                  pl.BlockSpec((1, 1, d), lambda b, i: (mod_row(b, i), 0, 0)),
                  pl.BlockSpec((1, 1, d), lambda b, i: (mod_row(b, i), 0, 1)),
                  pl.BlockSpec((1, d), lambda b, i: (0, 0)),
                  pl.BlockSpec((d, _C_END), lambda b, i: (0, 0)),
                  pl.BlockSpec((LANES, 2 * GLA_K), lambda b, i: (0, 0)),
                  pl.BlockSpec((1, 2 * GLA_K), lambda b, i: (0, 0)),
                  pl.BlockSpec((tm, LANES), lambda b, i: (i, 0)),
                  pl.BlockSpec((tm, LANES), lambda b, i: (i, 0))],
        out_specs=[tile(w) for w, _ in outs],
        out_shape=[jax.ShapeDtypeStruct((bsz, seq, w), dt) for w, dt in outs],
        name="premix",
        compiler_params=_cparams(("parallel", "parallel")),
    )(xall, mod, mod, norm1, w_all, wa2, ba, cos_t, sin_t)


def _gla_kernel(qf_ref, kf_ref, vf_ref, gf_ref, qb_ref, kb_ref, vb_ref, gb_ref,
                of_ref, ob_ref, st_ref, *, bsz, rows):
    @pl.when(pl.program_id(0) == 0)
    def _():
        st_ref[...] = jnp.zeros_like(st_ref)

    n_sub = rows // SUB
    rr = lax.broadcasted_iota(jnp.int32, (rows, rows), 0)
    cc = lax.broadcasted_iota(jnp.int32, (rows, rows), 1)
    same_sub = rr // SUB == cc // SUB
    tri = ((same_sub & (cc <= rr)).astype(BF16), (same_sub & (cc >= rr)).astype(BF16))
    t_idx = lax.broadcasted_iota(jnp.int32, (SUB, 1), 0)
    same_head = (lax.broadcasted_iota(jnp.int32, (GLA_V, GLA_K), 0) // GLA_DV
                 == lax.broadcasted_iota(jnp.int32, (GLA_V, GLA_K), 1) // GLA_DK)
    expand = (lax.broadcasted_iota(jnp.int32, (GLA_K, GLA_V), 0) // GLA_DK
              == lax.broadcasted_iota(jnp.int32, (GLA_K, GLA_V), 1) // GLA_DV).astype(BF16)
    chains = [(b, 0, qf_ref, kf_ref, vf_ref, gf_ref, of_ref) for b in range(bsz)]
    chains += [(b, 1, qb_ref, kb_ref, vb_ref, gb_ref, ob_ref) for b in range(bsz)]

    for b, rev, q_ref, k_ref, v_ref, g_ref, o_ref in chains:
        g_hi, g_lo = _split_bf16(g_ref[b] * LOG2_E)
        bloc_all = _dot(tri[rev], g_hi) + _dot(tri[rev], g_lo)
        order = range(n_sub - 1, -1, -1) if rev else range(n_sub)
        parts = {}
        for j in order:
            rs = slice(j * SUB, (j + 1) * SUB)
            q = q_ref[b, rs, :]
            k = k_ref[b, rs, :]
            vb = v_ref[b, rs, :]
            v = vb.astype(F32)
            bloc = bloc_all[rs]
            bend = bloc[0:1] if rev else bloc[SUB - 1:SUB]
            qk_terms = []
            for s in range(SUB):
                valid = (t_idx <= s) if rev else (t_idx >= s)
                rel = jnp.where(valid, bloc - bloc[s:s + 1], -jnp.inf)
                qk_terms.append((q * k[s:s + 1] * jnp.exp2(rel)).astype(BF16))
            att = _dot(jnp.concatenate(qk_terms, axis=0), expand)
            o_diag = att[0:SUB] * v[0:1]
            for s in range(1, SUB):
                o_diag = o_diag + att[s * SUB:(s + 1) * SUB] * v[s:s + 1]
            upd = _dot_tn(vb, (k * jnp.exp2(bend - bloc)).astype(BF16))
            parts[j] = ((q * jnp.exp2(bloc)).astype(BF16), o_diag, jnp.exp2(bend), jnp.where(same_head, upd, 0.0))
        st = st_ref[2 * b + rev]
        for j in order:
            q_dec, o_diag, decay, upd = parts[j]
            o_ref[b, j * SUB:(j + 1) * SUB, :] = _dot_nt(q_dec, st.astype(BF16)) + o_diag
            st = st * decay + upd
        st_ref[2 * b + rev] = st


def _gla(qg, kg, vg, la, n_ctx, rows):
    bsz, seq, _ = qg.shape
    nc = n_ctx // rows
    nblk = seq // rows

    def fwd(i):
        return (0, i, 0)

    def bwd_blk(i):
        return jnp.where(i < nc, nc - 1 - i, nblk + nc - 1 - i)

    def bwd(i):
        return (0, bwd_blk(i), 0)

    def spec(w, imap):
        return pl.BlockSpec((bsz, rows, w), imap)

    return pl.pallas_call(
        functools.partial(_gla_kernel, bsz=bsz, rows=rows),
        grid=(nblk,),
        in_specs=[spec(GLA_K, fwd), spec(GLA_K, fwd), spec(GLA_V, fwd), spec(GLA_K, fwd),
                  spec(GLA_K, bwd), spec(GLA_K, bwd), spec(GLA_V, bwd),
                  spec(GLA_K, lambda i: (0, bwd_blk(i), 1))],
        out_specs=[spec(GLA_V, fwd), spec(GLA_V, bwd)],
        out_shape=[jax.ShapeDtypeStruct((bsz, seq, GLA_V), F32)] * 2,
        scratch_shapes=[pltpu.VMEM((2 * bsz, GLA_V, GLA_K), F32)],
        name="gla_scan",
        compiler_params=_cparams(("arbitrary",)),
    )(qg, kg, vg, la, qg, kg, vg, la)


def _attn_kernel(lq1_ref, lk1_ref, lq2_ref, lk2_ref, q_ref, k_ref, v_ref, o_ref,
                 m_ref, acc_ref, sa_ref, sb_ref, ma_ref, mb_ref, *, n_chunks, tk, lam_init):
    lam = (jnp.exp(jnp.sum(lq1_ref[...] * lk1_ref[...], axis=-1, keepdims=True))
           - jnp.exp(jnp.sum(lq2_ref[...] * lk2_ref[...], axis=-1, keepdims=True)) + lam_init)
    q = q_ref[0]
    lane = lax.broadcasted_iota(jnp.int32, (1, LANES), 1)
    zero = jnp.zeros_like(q)
    qs = (jnp.where(lane < DIFF_DH, q, zero), jnp.where(lane >= DIFF_DH, q, zero))
    m_ref[...] = jnp.full_like(m_ref, -jnp.inf)
    acc_ref[...] = jnp.zeros_like(acc_ref)

    def keys(j):
        return pl.ds(pl.multiple_of(j * tk, tk), tk)

    def scores(j, s_ref, mx_ref):
        k = k_ref[0, keys(j), :]
        for mp in range(2):
            s = _dot_nt(qs[mp], k)
            s_ref[mp] = s
            mx_ref[mp] = jnp.max(s, axis=-1, keepdims=True)

    def consume(j, s_ref, mx_ref):
        v_ext = jnp.concatenate([v_ref[0, keys(j), :], jnp.ones((tk, LANES), BF16)], axis=1)
        for mp in range(2):
            m_old = m_ref[mp]
            m_new = jnp.maximum(m_old, mx_ref[mp])
            p = jnp.exp2(s_ref[mp] - m_new).astype(BF16)
            acc_ref[mp] = jnp.exp2(m_old - m_new) * acc_ref[mp] + _dot(p, v_ext)
            m_ref[mp] = m_new

    buf_a, buf_b = (sa_ref, ma_ref), (sb_ref, mb_ref)
    scores(0, *buf_a)

    def chunk_pair(jj, carry):
        scores(2 * jj + 1, *buf_b)
        consume(2 * jj, *buf_a)
        scores(2 * jj + 2, *buf_a)
        consume(2 * jj + 1, *buf_b)
        return carry

    lax.fori_loop(0, n_chunks // 2, chunk_pair, 0)
    consume(n_chunks - 1, *buf_a)
    o_ref[0] = (acc_ref[0, :, 0:LANES] / acc_ref[0, :, LANES:]
                - lam * (acc_ref[1, :, 0:LANES] / acc_ref[1, :, LANES:]))


def _key_chunk(n_kv, target):
    sizes = [c for c in range(LANES, n_kv + 1, LANES) if n_kv % c == 0 and (n_kv // c) % 2 == 1]
    return max([c for c in sizes if c <= target] or sizes[:1])


def _attention(qd, kd, vd, lam_vecs, lam_init, n_q, n_kv, tq):
    bsz = qd.shape[0]
    tk = _key_chunk(n_kv, KEY_CHUNK_TARGET)
    n_chunks = n_kv // tk
    assert n_chunks % 2 == 1 and n_q % tq == 0
    lam_spec = pl.BlockSpec((1, DIFF_DH), lambda b, h, i: (0, 0))
    return pl.pallas_call(
        functools.partial(_attn_kernel, n_chunks=n_chunks, tk=tk, lam_init=lam_init),
        grid=(bsz, DIFF_HEADS, n_q // tq),
        in_specs=[lam_spec] * 4 + [
            pl.BlockSpec((1, tq, LANES), lambda b, h, i: (b, i, h)),
            pl.BlockSpec((1, n_kv, LANES), lambda b, h, i: (b, 0, h)),
            pl.BlockSpec((1, n_kv, LANES), lambda b, h, i: (b, 0, h))],
        out_specs=pl.BlockSpec((1, tq, LANES), lambda b, h, i: (b, i, h)),
        out_shape=jax.ShapeDtypeStruct((bsz, n_q, DIFF_V), F32),
        scratch_shapes=[pltpu.VMEM((2, tq, 1), F32), pltpu.VMEM((2, tq, 2 * LANES), F32),
                        pltpu.VMEM((2, tq, tk), F32), pltpu.VMEM((2, tq, tk), F32),
                        pltpu.VMEM((2, tq, 1), F32), pltpu.VMEM((2, tq, 1), F32)],
        name="diff_attn",
        compiler_params=_cparams(("parallel", "parallel", "arbitrary")),
    )(*lam_vecs, qd, kd, vd)


def _postmix_kernel(x_ref, of_ref, ob_ref, og_ref, odc_ref, odl_ref, pc_ref, pp_ref, pn_ref,
                    gn_ref, dn_ref, pw_ref, ps_ref, wo_ref, g1_ref, sh2_ref, sc2_ref, n2_ref,
                    wr_ref, br_ref, x1_ref, h2_ref, route_ref, ext_ref,
                    *, tm, nct, n_ctx, seq, lam_init):
    i = pl.program_id(1)

    a = of_ref[0] + ob_ref[0]
    avg = ((lax.broadcasted_iota(jnp.int32, (GLA_V, GLA_V), 0) // GLA_DV
            == lax.broadcasted_iota(jnp.int32, (GLA_V, GLA_V), 1) // GLA_DV).astype(F32) * (1.0 / GLA_DV)).astype(BF16)
    sq_hi, sq_lo = _split_bf16(a * a)
    gla = a * lax.rsqrt(_dot(sq_hi, avg) + _dot(sq_lo, avg) + EPS) * gn_ref[...] * _silu(og_ref[0])
    y = _dot(gla.astype(BF16), wo_ref[0:GLA_V, :])

    for hd in range(DIFF_HEADS):
        lo = hd * DIFF_DV
        od = jnp.where(i < nct, odc_ref[0, :, lo:lo + DIFF_DV], odl_ref[0, :, lo:lo + DIFF_DV])
        dh = _rms(od) * dn_ref[:, lo:lo + DIFF_DV] * (1.0 - lam_init)
        y = y + _dot(dh.astype(BF16), wo_ref[GLA_V + lo:GLA_V + lo + DIFF_DV, :])

    seg_lo = jnp.where(i < nct, 0, n_ctx)
    seg_hi = jnp.where(i < nct, n_ctx, seq)
    ext_ref[0:POOL_HALO] = pp_ref[0]
    ext_ref[POOL_HALO:POOL_HALO + tm] = pc_ref[0]
    ext_ref[POOL_HALO + tm:] = pn_ref[0]
    pos_e = i * tm - POOL_HALO + lax.broadcasted_iota(jnp.int32, (tm + 2 * POOL_HALO, 1), 0)
    e = jnp.where((pos_e >= seg_lo) & (pos_e < seg_hi), ext_ref[...], 0.0)
    pos = i * tm + lax.broadcasted_iota(jnp.int32, (tm, 1), 0)
    grp = lax.broadcasted_iota(jnp.int32, (1, POOL_W), 1) // POOL_CH
    run, width, mean = e, 1, jnp.zeros((tm, POOL_W), F32)
    for gi, w in enumerate(POOL_WINDOWS):
        while width < w:
            n = run.shape[0] - width
            run = run[0:n] + run[width:width + n]
            width *= 2
        start = POOL_HALO - w // 2
        cnt = (jnp.minimum(pos + (w - w // 2), seg_hi) - jnp.maximum(pos - w // 2, seg_lo)).astype(F32)
        mean = jnp.where(grp == gi, run[start:start + tm] / cnt, mean)
    pooled = _dot((mean - pc_ref[0]).astype(BF16), pw_ref[...]) * ps_ref[...]
    y = y + _dot(pooled.astype(BF16), wo_ref[GLA_V + DIFF_V:, :])

    x1 = x_ref[0] + g1_ref[0] * y
    x1_ref[0] = x1
    h2 = _rms(x1) * n2_ref[...] * (1.0 + sc2_ref[0]) + sh2_ref[0]
    bits = lax.bitcast_convert_type(h2.astype(BF16).astype(F32), jnp.uint32)
    half = bits.shape[1] // 2
    h2_ref[0] = (bits[:, 0:half] >> 16) | (bits[:, half:] & jnp.uint32(HI16))

    h_hi, h_lo = _split_bf16(h2)
    both = _dot(h_hi, wr_ref[...])
    logit = both[:, 0:ROUTE_W] + both[:, ROUTE_W:] + _dot(h_lo, wr_ref[:, 0:ROUTE_W]) + br_ref[...]
    lane = lax.broadcasted_iota(jnp.int32, (1, ROUTE_W), 1).astype(F32)
    neg = -jnp.inf

    def top(vals):
        mx = jnp.max(vals, axis=-1, keepdims=True)
        idx = jnp.min(jnp.where(vals == mx, lane, float(ROUTE_W)), axis=-1, keepdims=True)
        return mx, idx

    gl = jnp.where(lane < N_GROUPS, logit, neg)
    gmax, gidx = top(gl)
    g_top = 1.0 / jnp.sum(jnp.exp(gl - gmax), axis=-1, keepdims=True)
    e_lo = N_GROUPS + gidx * EXPERTS_PER_GROUP
    el = jnp.where((lane >= e_lo) & (lane < e_lo + EXPERTS_PER_GROUP), logit, neg)
    emax, idx1 = top(el)
    esum = jnp.sum(jnp.exp(el - emax), axis=-1, keepdims=True)
    emax2, idx2 = top(jnp.where(lane == idx1, neg, el))
    e1 = 1.0 / esum
    e2 = jnp.exp(emax2 - emax) / esum
    w1 = g_top * e1 / (e1 + e2)
    w2 = g_top * e2 / (e1 + e2)
    rec = jnp.where(lane == 0, idx1 - N_GROUPS, 0.0)
    rec = jnp.where(lane == 1, idx2 - N_GROUPS, rec)
    rec = jnp.where(lane == 2, w1, rec)
    route_ref[0] = jnp.where(lane == 3, w2, rec)


def _postmix(xall, o_f, o_b, og, od_ctx, od_lat, pool, mod, prm, nct, n_ctx, tm, lam_init):
    bsz, seq, d = xall.shape
    hpb = tm // POOL_HALO
    n_halo = seq // POOL_HALO

    def mod_spec(col):
        return pl.BlockSpec((1, 1, d), lambda b, i: (jnp.where(i < nct, bsz, b), 0, col))

    tile = lambda w: pl.BlockSpec((1, tm, w), lambda b, i: (b, i, 0))
    full = lambda r, c: pl.BlockSpec((r, c), lambda b, i: (0, 0))
    return pl.pallas_call(
        functools.partial(_postmix_kernel, tm=tm, nct=nct, n_ctx=n_ctx, seq=seq, lam_init=lam_init),
        grid=(bsz, seq // tm),
        in_specs=[tile(d), tile(GLA_V), tile(GLA_V), tile(GLA_V),
                  pl.BlockSpec((1, tm, DIFF_V), lambda b, i: (b, jnp.minimum(i, nct - 1), 0)),
                  pl.BlockSpec((1, tm, DIFF_V), lambda b, i: (b, jnp.maximum(i - nct, 0), 0)),
                  tile(POOL_W),
                  pl.BlockSpec((1, POOL_HALO, POOL_W), lambda b, i: (b, jnp.maximum(i * hpb - 1, 0), 0)),
                  pl.BlockSpec((1, POOL_HALO, POOL_W), lambda b, i: (b, jnp.minimum((i + 1) * hpb, n_halo - 1), 0)),
                  full(1, GLA_V), full(1, DIFF_V), full(POOL_W, POOL_W), full(1, POOL_W), full(d, d),
                  mod_spec(2), mod_spec(3), mod_spec(4), full(1, d), full(d, 2 * ROUTE_W), full(1, ROUTE_W)],
        out_specs=[tile(d), tile(d // 2), tile(ROUTE_W)],
        out_shape=[jax.ShapeDtypeStruct((bsz, seq, d), F32), jax.ShapeDtypeStruct((bsz, seq, d // 2), jnp.uint32),
                   jax.ShapeDtypeStruct((bsz, seq, ROUTE_W), F32)],
        scratch_shapes=[pltpu.VMEM((tm + 2 * POOL_HALO, POOL_W), F32)],
        name="postmix",
        compiler_params=_cparams(("parallel", "parallel")),
    )(xall, o_f, o_b, og, od_ctx, od_lat, pool, pool, pool, prm["gla_norm"], prm["diff_norm"], prm["pool_w"],
      prm["pool_scale"], prm["w_out"], mod, mod, mod, prm["norm2"], prm["w_route"], prm["b_route"])


def _slot_of_assignment(expert, n_tok):
    n_assign = n_tok * TOP_K
    e = expert.reshape(n_assign)
    hot = (e[:, None] == jnp.arange(N_EXPERTS, dtype=jnp.int32)[None, :]).astype(jnp.int32)
    csum = jnp.cumsum(hot, axis=0)
    counts = csum[-1]
    rank = jnp.sum(csum * hot, axis=1) - 1
    padded = (counts + MOE_BLOCK - 1) // MOE_BLOCK * MOE_BLOCK
    padded_end = jnp.cumsum(padded)
    dest = (padded_end - padded)[e] + rank
    n_slots = -(-n_assign // MOE_BLOCK) * MOE_BLOCK + N_EXPERTS * MOE_BLOCK
    block_start = jnp.arange(n_slots // MOE_BLOCK, dtype=jnp.int32) * MOE_BLOCK
    block_expert = jnp.minimum(jnp.sum((padded_end[None, :] <= block_start[:, None]).astype(jnp.int32), axis=1),
                               N_EXPERTS - 1)
    return dest, block_expert, n_slots


def _each(n, fn):
    def step(r, carry):
        fn(r)
        return carry
    lax.fori_loop(0, n, step, 0, unroll=DMA_UNROLL)


def _dispatch_kernel(dcur_ref, dprev_ref, h_ref, xs_in, xs_hbm, buf, sem, *, tile):
    del xs_in
    i = pl.program_id(0)
    n_tiles = pl.num_programs(0) - 1

    def copy(step, dest_ref, r, k):
        return pltpu.make_async_copy(buf.at[step & 1, pl.ds(r, 1), :],
                                     xs_hbm.at[pl.ds(dest_ref[0, 0, r * TOP_K + k], 1), :], sem.at[step & 1])

    @pl.when(i < n_tiles)
    def _():
        buf[i & 1] = h_ref[...]
        _each(tile, lambda r: [copy(i, dcur_ref, r, k).start() for k in range(TOP_K)])

    @pl.when(i > 0)
    def _():
        _each(tile, lambda r: [copy(i - 1, dprev_ref, r, k).wait() for k in range(TOP_K)])


def _dispatch(h2p, dest, n_slots, tile):
    n_tok, w = h2p.shape
    n_tiles = n_tok // tile
    dest3 = dest.reshape(n_tiles, 1, tile * TOP_K)
    smem = lambda imap: pl.BlockSpec((1, 1, tile * TOP_K), imap, memory_space=pltpu.SMEM)
    return pl.pallas_call(
        functools.partial(_dispatch_kernel, tile=tile),
        grid=(n_tiles + 1,),
        in_specs=[smem(lambda i: (jnp.minimum(i, n_tiles - 1), 0, 0)), smem(lambda i: (jnp.maximum(i - 1, 0), 0, 0)),
                  pl.BlockSpec((tile, w), lambda i: (jnp.minimum(i, n_tiles - 1), 0)),
                  pl.BlockSpec(memory_space=pl.ANY)],
        out_specs=pl.BlockSpec(memory_space=pl.ANY),
        out_shape=jax.ShapeDtypeStruct((n_slots, w), jnp.uint32),
        scratch_shapes=[pltpu.VMEM((2, tile, w), jnp.uint32), pltpu.SemaphoreType.DMA((2,))],
        input_output_aliases={3: 0},
        name="moe_dispatch",
        compiler_params=_cparams(("arbitrary",)),
    )(dest3, dest3, h2p, jnp.zeros((n_slots, w), jnp.uint32))


def _expert_kernel(be_ref, xs_ref, w1_ref, w3_ref, w2_ref, ys_ref, w1b_ref, w3b_ref, w2b_ref):
    i = pl.program_id(0)

    @pl.when((i == 0) | (be_ref[i] != be_ref[jnp.maximum(i - 1, 0)]))
    def _():
        w1b_ref[...] = w1_ref[0, 0].astype(BF16)
        w3b_ref[...] = w3_ref[0, 0].astype(BF16)
        w2b_ref[...] = w2_ref[0, 0].astype(BF16)

    bits = xs_ref[...]
    half = bits.shape[1]
    x_lo = lax.bitcast_convert_type(bits << 16, F32).astype(BF16)
    x_hi = lax.bitcast_convert_type(bits & jnp.uint32(HI16), F32).astype(BF16)

    def up(w_ref):
        return _dot(x_lo, w_ref[0:half, :]) + _dot(x_hi, w_ref[half:, :])

    ys_ref[...] = _dot((_silu(up(w1b_ref)) * up(w3b_ref)).astype(BF16), w2b_ref[...])


def _experts(xs, block_expert, w1, w3, w2, layer):
    n_slots, half = xs.shape
    _, _, d, d_exp = w1.shape
    return pl.pallas_call(
        _expert_kernel,
        grid_spec=pltpu.PrefetchScalarGridSpec(
            num_scalar_prefetch=1,
            grid=(n_slots // MOE_BLOCK,),
            in_specs=[pl.BlockSpec((MOE_BLOCK, half), lambda i, be: (i, 0)),
                      pl.BlockSpec((1, 1, d, d_exp), lambda i, be: (layer, be[i], 0, 0)),
                      pl.BlockSpec((1, 1, d, d_exp), lambda i, be: (layer, be[i], 0, 0)),
                      pl.BlockSpec((1, 1, d_exp, d), lambda i, be: (layer, be[i], 0, 0))],
            out_specs=pl.BlockSpec((MOE_BLOCK, d), lambda i, be: (i, 0)),
            scratch_shapes=[pltpu.VMEM((d, d_exp), BF16), pltpu.VMEM((d, d_exp), BF16),
                            pltpu.VMEM((d_exp, d), BF16)]),
        out_shape=jax.ShapeDtypeStruct((n_slots, d), F32),
        name="moe_experts",
        compiler_params=_cparams(("arbitrary",)),
    )(block_expert, xs, w1, w3, w2)


def _combine_kernel(dcur_ref, dnext_ref, x1_ref, route_ref, g2_ref, fn_ref, ys_hbm, o_ref, ybuf, sem,
                    *, final, tile):
    i = pl.program_id(0)
    n = pl.num_programs(0)

    def fetch(step, dest_ref, r, k):
        slot = step & 1
        return pltpu.make_async_copy(ys_hbm.at[pl.ds(dest_ref[0, 0, r * TOP_K + k], 1), :],
                                     ybuf.at[slot, k, pl.ds(r, 1), :], sem.at[slot])

    @pl.when(i == 0)
    def _():
        _each(tile, lambda r: [fetch(i, dcur_ref, r, k).start() for k in range(TOP_K)])

    @pl.when(i + 1 < n)
    def _():
        _each(tile, lambda r: [fetch(i + 1, dnext_ref, r, k).start() for k in range(TOP_K)])

    _each(tile, lambda r: [fetch(i, dcur_ref, r, k).wait() for k in range(TOP_K)])
    route = route_ref[0]
    lane = lax.broadcasted_iota(jnp.int32, (1, ROUTE_W), 1)
    w0 = jnp.sum(jnp.where(lane == 2, route, 0.0), axis=-1, keepdims=True)
    w1 = jnp.sum(jnp.where(lane == 3, route, 0.0), axis=-1, keepdims=True)
    x = x1_ref[0] + g2_ref[0] * (ybuf[i & 1, 0] * w0 + ybuf[i & 1, 1] * w1)
    o_ref[0] = _rms(x) * fn_ref[...] if final else x


def _combine(x1, ys, dest, route, mod, final_norm, nct, tm, final):
    bsz, seq, d = x1.shape
    off = nct if final else 0
    tpb = seq // tm - off
    per_b = seq // tm

    def tok_tile(i):
        return (i // tpb) * per_b + off + i % tpb

    n_steps = bsz * tpb
    dest3 = dest.reshape(bsz * per_b, 1, tm * TOP_K)
    smem = lambda imap: pl.BlockSpec((1, 1, tm * TOP_K), imap, memory_space=pltpu.SMEM)
    tile = lambda w: pl.BlockSpec((1, tm, w), lambda i: (i // tpb, off + i % tpb, 0))
    return pl.pallas_call(
        functools.partial(_combine_kernel, final=final, tile=tm),
        grid=(n_steps,),
        in_specs=[smem(lambda i: (tok_tile(i), 0, 0)),
                  smem(lambda i: (tok_tile(jnp.minimum(i + 1, n_steps - 1)), 0, 0)),
                  tile(d), tile(ROUTE_W),
                  pl.BlockSpec((1, 1, d), lambda i: (jnp.where(off + i % tpb < nct, bsz, i // tpb), 0, 5)),
                  pl.BlockSpec((1, d), lambda i: (0, 0)),
                  pl.BlockSpec(memory_space=pl.ANY)],
        out_specs=pl.BlockSpec((1, tm, d), lambda i: (i // tpb, i % tpb, 0)),
        out_shape=jax.ShapeDtypeStruct((bsz, tpb * tm, d), F32),
        scratch_shapes=[pltpu.VMEM((2, TOP_K, tm, d), F32), pltpu.SemaphoreType.DMA((2,))],
        name="combine",
        compiler_params=_cparams(("arbitrary",)),
    )(dest3, dest3, x1, route, mod, final_norm, ys)


def _rope_tables(n_ctx, n_lat):
    t = jnp.arange(n_lat, dtype=jnp.int32)
    row = (t // GRID_W).astype(F32)
    col = (t % GRID_W).astype(F32)
    inv = 1.0 / (ROPE_BASE ** (jnp.arange(0, AX_DIM, 2, dtype=F32) / AX_DIM))
    lane = jnp.arange(LANES)
    within = lane % DIFF_DH
    pos = jnp.where((within < AX_DIM)[None, :], row[:, None], col[:, None])
    ang = pos * inv[within % (AX_DIM // 2)][None, :]
    sign = jnp.where((within % AX_DIM) < AX_DIM // 2, -1.0, 1.0)[None, :]
    cos = jnp.concatenate([jnp.ones((n_ctx, LANES), F32), jnp.cos(ang)], axis=0)
    sin = jnp.concatenate([jnp.zeros((n_ctx, LANES), F32), jnp.sin(ang) * sign], axis=0)
    return cos, sin


def _pack_layer(layer, w_in, w_out, wa2_f, ba_f, wa2_b, ba_b, pool_w, wg, bg, we, be):
    d = w_in.shape[1]
    wi = w_in[layer]
    o = 0
    parts = {}
    for name, size in (("qg", GLA_K), ("kg", GLA_K), ("vg", GLA_V), ("og", GLA_V), ("af", GATE_RANK),
                       ("ab", GATE_RANK), ("qd", DIFF_QK), ("kd", DIFF_QK), ("vd", DIFF_V), ("pl", POOL_W)):
        parts[name] = wi[:, o:o + size]
        o += size
    gate = jnp.concatenate([parts["af"], parts["ab"], jnp.zeros((d, LANES - 2 * GATE_RANK), F32)], axis=1)
    w_all = jnp.concatenate([parts[n] for n in ("qg", "kg", "vg", "og", "qd", "kd", "vd", "pl")] + [gate],
                            axis=1).astype(BF16)
    wa2 = jnp.zeros((LANES, 2 * GLA_K), F32)
    wa2 = wa2.at[0:GATE_RANK, 0:GLA_K].set(wa2_f[layer])
    wa2 = wa2.at[GATE_RANK:2 * GATE_RANK, GLA_K:].set(wa2_b[layer])
    ba = jnp.concatenate([ba_f[layer], ba_b[layer]])[None, :]
    pw = jnp.zeros((POOL_W, POOL_W), F32)
    for gi in range(len(POOL_WINDOWS)):
        pw = pw.at[gi * POOL_CH:(gi + 1) * POOL_CH, gi * POOL_CH:(gi + 1) * POOL_CH].set(pool_w[layer, gi])
    w_route = jnp.concatenate([wg[layer], we[layer], jnp.zeros((d, ROUTE_W - N_GROUPS - N_EXPERTS), F32)], axis=1)
    b_route = jnp.concatenate([bg[layer], be[layer], jnp.zeros((ROUTE_W - N_GROUPS - N_EXPERTS,), F32)])[None, :]
    head = lax.bitcast_convert_type(lax.bitcast_convert_type(w_route, jnp.uint32) & jnp.uint32(HI16), F32)
    w_route = jnp.concatenate([head.astype(BF16), (w_route - head).astype(BF16)], axis=1)
    return dict(w_all=w_all, wa2=wa2, ba=ba, pool_w=pw.astype(BF16), w_out=w_out[layer].astype(BF16),
                w_route=w_route, b_route=b_route)


def kernel(x, c, ctx, c_ctx, w_mod, b_mod, norm1, norm2, w_in, w_out, gla_wa2_f, gla_ba_f, gla_wa2_b, gla_ba_b, gla_norm, lam_q1, lam_k1, lam_q2, lam_k2, diff_norm, pool_w, pool_scale, router_wg, router_bg, router_we, router_be, exp_w1, exp_w3, exp_w2, final_norm):
    bsz, n_lat, d = x.shape
    n_ctx = ctx.shape[1]
    depth = w_mod.shape[0]
    seq = n_ctx + n_lat
    tm = math.gcd(TOKEN_TILE, n_ctx)
    nct = n_ctx // tm
    tq = math.gcd(QUERY_TILE, n_lat)
    gla_rows = math.gcd(GLA_ROWS, n_ctx)
    assert bsz + 1 <= SUBLANES and n_lat % tm == 0 and n_lat % GRID_W == 0

    cond = jnp.concatenate([c, c_ctx[None, :], jnp.zeros((SUBLANES - bsz - 1, d), F32)], axis=0)
    mod_all = _adaln(cond, w_mod, b_mod)
    cos_t, sin_t = _rope_tables(n_ctx, n_lat)
    xall = jnp.concatenate([ctx, x], axis=1)
    n_tok = bsz * seq

    for layer in range(depth):
        last = layer == depth - 1
        lam_init = 0.8 - 0.6 * math.exp(-0.3 * layer)
        prm = _pack_layer(layer, w_in, w_out, gla_wa2_f, gla_ba_f, gla_wa2_b, gla_ba_b, pool_w,
                          router_wg, router_bg, router_we, router_be)
        prm.update(gla_norm=gla_norm[layer][None, :], diff_norm=diff_norm[layer][None, :],
                   pool_scale=pool_scale[layer][None, :], norm2=norm2[layer][None, :])
        mod = mod_all[layer].reshape(SUBLANES, 1, 6 * d)

        qg, kg, vg, og, la, qd, kd, vd, pool = _premix(
            xall, mod, norm1[layer][None, :], prm["w_all"], prm["wa2"], prm["ba"], cos_t, sin_t, nct, tm)
        o_f, o_b = _gla(qg, kg, vg, la, n_ctx, gla_rows)
        lam_vecs = [v[layer][None, :] for v in (lam_q1, lam_k1, lam_q2, lam_k2)]
        od_lat = _attention(qd[:, n_ctx:], kd, vd, lam_vecs, lam_init, n_lat, seq, tq)
        od_ctx = _attention(qd, kd, vd, lam_vecs, lam_init, n_ctx, n_ctx, tm)
        x1, h2p, route = _postmix(xall, o_f, o_b, og, od_ctx, od_lat, pool, mod, prm, nct, n_ctx, tm, lam_init)

        expert = route[..., 0:TOP_K].astype(jnp.int32).reshape(n_tok, TOP_K)
        dest, block_expert, n_slots = _slot_of_assignment(expert, n_tok)
        xs = _dispatch(h2p.reshape(n_tok, d // 2), dest, n_slots, tm)
        ys = _experts(xs, block_expert, exp_w1, exp_w3, exp_w2, layer)
        xall = _combine(x1, ys, dest, route, mod, final_norm[None, :], nct, tm, last)
    return xall
```

```python
import functools
import math

import jax
import jax.numpy as jnp
from jax import lax
from jax.experimental import pallas as pl
from jax.experimental.pallas import tpu as pltpu

F32 = jnp.float32
BF16 = jnp.bfloat16
HIGHEST = lax.Precision.HIGHEST

EPS = 1e-6
GRID_W = 64
GLA_HEADS, GLA_DK, GLA_DV = 4, 32, 64
GLA_K, GLA_V = GLA_HEADS * GLA_DK, GLA_HEADS * GLA_DV
GATE_RANK, GATE_TEMP = 16, 16.0
DIFF_HEADS, DIFF_DH = 4, 64
DIFF_DV = 2 * DIFF_DH
DIFF_QK = DIFF_HEADS * 2 * DIFF_DH
DIFF_V = DIFF_HEADS * DIFF_DV
LOG2_E = math.log2(math.e)
Q_SCALE = DIFF_DH ** -0.5 * LOG2_E
ROPE_BASE = 10000.0
AX_DIM = DIFF_DH // 2
POOL_WINDOWS = (2, 4, 8, 16)
POOL_CH = 64
POOL_W = len(POOL_WINDOWS) * POOL_CH
POOL_HALO = max(POOL_WINDOWS) // 2
N_GROUPS, EXPERTS_PER_GROUP = 4, 4
N_EXPERTS = N_GROUPS * EXPERTS_PER_GROUP
TOP_K = 2
MOE_BLOCK = 256

LANES = 128
SUBLANES = 8
SUB = 16
GLA_ROWS = 128
ROUTE_W = LANES
TOKEN_TILE = 256
QUERY_TILE = 512
KEY_CHUNK_TARGET = 1280
ADALN_COLS = 1536
DMA_UNROLL = 8
HI16 = 0xFFFF0000
VMEM_LIMIT = 56 * 1024 * 1024


def _cparams(sem):
    return pltpu.CompilerParams(dimension_semantics=sem, vmem_limit_bytes=VMEM_LIMIT)


def _dot(a, b):
    return jnp.dot(a, b, preferred_element_type=F32)


def _dot_hi(a, b):
    return jnp.dot(a, b, precision=HIGHEST, preferred_element_type=F32)


def _dot_nt(a, b):
    return lax.dot_general(a, b, (((1,), (1,)), ((), ())), preferred_element_type=F32)


def _dot_tn(a, b):
    return lax.dot_general(a, b, (((0,), (0,)), ((), ())), preferred_element_type=F32)


def _split_bf16(x):
    hi = x.astype(BF16)
    return hi, (x - hi.astype(F32)).astype(BF16)


def _silu(x):
    return x * jax.nn.sigmoid(x)


def _log_sigmoid(x):
    return jnp.minimum(x, 0.0) - jnp.log1p(jnp.exp(-jnp.abs(x)))


def _rms(x):
    return x * lax.rsqrt(jnp.mean(x * x, axis=-1, keepdims=True) + EPS)


def _adaln_kernel(c_ref, w_ref, b_ref, o_ref):
    o_ref[0] = _dot_hi(_silu(c_ref[...]), w_ref[0]) + b_ref[0]


def _adaln(cond, w_mod, b_mod):
    depth, d, six_d = w_mod.shape
    tn = math.gcd(ADALN_COLS, six_d)
    return pl.pallas_call(
        _adaln_kernel,
        grid=(depth, six_d // tn),
        in_specs=[pl.BlockSpec((SUBLANES, d), lambda l, j: (0, 0)),
                  pl.BlockSpec((1, d, tn), lambda l, j: (l, 0, j)),
                  pl.BlockSpec((1, 1, tn), lambda l, j: (l, 0, j))],
        out_specs=pl.BlockSpec((1, SUBLANES, tn), lambda l, j: (l, 0, j)),
        out_shape=jax.ShapeDtypeStruct((depth, SUBLANES, six_d), F32),
        name="adaln",
        compiler_params=_cparams(("arbitrary", "arbitrary")),
    )(cond, w_mod, b_mod.reshape(depth, 1, six_d))


_C_QG, _C_KG, _C_VG, _C_OG = 0, GLA_K, 2 * GLA_K, 2 * GLA_K + GLA_V
_C_QD = 2 * GLA_K + 2 * GLA_V
_C_KD = _C_QD + DIFF_QK
_C_VD = _C_KD + DIFF_QK
_C_PL = _C_VD + DIFF_V
_C_GT = _C_PL + POOL_W
_C_END = _C_GT + LANES


def _premix_kernel(x_ref, sh_ref, sc_ref, n1_ref, w_ref, wa2_ref, ba_ref, cos_ref, sin_ref,
                   qg_ref, kg_ref, vg_ref, og_ref, la_ref, qd_ref, kd_ref, vd_ref, pool_ref):
    x = x_ref[0]
    h = _rms(x) * n1_ref[...]
    hb = (h * (1.0 + sc_ref[0]) + sh_ref[0]).astype(BF16)

    def proj(lo, hi):
        return _dot(hb, w_ref[:, lo:hi])

    qg_ref[0] = proj(_C_QG, _C_KG) * (GLA_DK ** -0.5)
    kg_ref[0] = proj(_C_KG, _C_VG)
    vg_ref[0] = proj(_C_VG, _C_OG).astype(BF16)
    og_ref[0] = proj(_C_OG, _C_QD)
    vd_ref[0] = proj(_C_VD, _C_PL).astype(BF16)
    pool_ref[0] = proj(_C_PL, _C_GT)
    pre = _dot_hi(proj(_C_GT, _C_END), wa2_ref[...]) + ba_ref[...]
    la_ref[0] = _log_sigmoid(pre) / GATE_TEMP

    cos = cos_ref[...]
    sin = sin_ref[...]
    lane = lax.broadcasted_iota(jnp.int32, (1, LANES), 1)
    first_half = (lane % AX_DIM) < (AX_DIM // 2)

    def rope(a):
        partner = jnp.where(first_half, pltpu.roll(a, LANES - AX_DIM // 2, 1), pltpu.roll(a, AX_DIM // 2, 1))
        return a * cos + partner * sin

    for hd in range(DIFF_HEADS):
        lo = hd * LANES
        qd_ref[0, :, lo:lo + LANES] = (rope(proj(_C_QD + lo, _C_QD + lo + LANES)) * Q_SCALE).astype(BF16)
        kd_ref[0, :, lo:lo + LANES] = rope(proj(_C_KD + lo, _C_KD + lo + LANES)).astype(BF16)


def _premix(xall, mod, norm1, w_all, wa2, ba, cos_t, sin_t, nct, tm):
    bsz, seq, d = xall.shape

    def mod_row(b, i):
        return jnp.where(i < nct, bsz, b)

    tile = lambda w: pl.BlockSpec((1, tm, w), lambda b, i: (b, i, 0))
    outs = [(GLA_K, F32), (GLA_K, F32), (GLA_V, BF16), (GLA_V, F32), (2 * GLA_K, F32),
            (DIFF_QK, BF16), (DIFF_QK, BF16), (DIFF_V, BF16), (POOL_W, F32)]
    return pl.pallas_call(
        _premix_kernel,
        grid=(bsz, seq // tm),
        in_specs=[tile(d),
                  pl.BlockSpec((1, 1, d), lambda b, i: (mod_row(b, i), 0, 0)),
                  pl.BlockSpec((1, 1, d), lambda b, i: (mod_row(b, i), 0, 1)),
                  pl.BlockSpec((1, d), lambda b, i: (0, 0)),
                  pl.BlockSpec((d, _C_END), lambda b, i: (0, 0)),
                  pl.BlockSpec((LANES, 2 * GLA_K), lambda b, i: (0, 0)),
                  pl.BlockSpec((1, 2 * GLA_K), lambda b, i: (0, 0)),
                  pl.BlockSpec((tm, LANES), lambda b, i: (i, 0)),
                  pl.BlockSpec((tm, LANES), lambda b, i: (i, 0))],
        out_specs=[tile(w) for w, _ in outs],
        out_shape=[jax.ShapeDtypeStruct((bsz, seq, w), dt) for w, dt in outs],
        name="premix",
        compiler_params=_cparams(("parallel", "parallel")),
    )(xall, mod, mod, norm1, w_all, wa2, ba, cos_t, sin_t)


def _gla_kernel(qf_ref, kf_ref, vf_ref, gf_ref, qb_ref, kb_ref, vb_ref, gb_ref,
                of_ref, ob_ref, st_ref, *, bsz, rows):
    @pl.when(pl.program_id(0) == 0)
    def _():
        st_ref[...] = jnp.zeros_like(st_ref)

    n_sub = rows // SUB
    rr = lax.broadcasted_iota(jnp.int32, (rows, rows), 0)
    cc = lax.broadcasted_iota(jnp.int32, (rows, rows), 1)
    same_sub = rr // SUB == cc // SUB
    tri = ((same_sub & (cc <= rr)).astype(BF16), (same_sub & (cc >= rr)).astype(BF16))
    t_idx = lax.broadcasted_iota(jnp.int32, (SUB, 1), 0)
    same_head = (lax.broadcasted_iota(jnp.int32, (GLA_V, GLA_K), 0) // GLA_DV
                 == lax.broadcasted_iota(jnp.int32, (GLA_V, GLA_K), 1) // GLA_DK)
    expand = (lax.broadcasted_iota(jnp.int32, (GLA_K, GLA_V), 0) // GLA_DK
              == lax.broadcasted_iota(jnp.int32, (GLA_K, GLA_V), 1) // GLA_DV).astype(BF16)
    chains = [(b, 0, qf_ref, kf_ref, vf_ref, gf_ref, of_ref) for b in range(bsz)]
    chains += [(b, 1, qb_ref, kb_ref, vb_ref, gb_ref, ob_ref) for b in range(bsz)]

    for b, rev, q_ref, k_ref, v_ref, g_ref, o_ref in chains:
        g_hi, g_lo = _split_bf16(g_ref[b] * LOG2_E)
        bloc_all = _dot(tri[rev], g_hi) + _dot(tri[rev], g_lo)
        order = range(n_sub - 1, -1, -1) if rev else range(n_sub)
        parts = {}
        for j in order:
            rs = slice(j * SUB, (j + 1) * SUB)
            q = q_ref[b, rs, :]
            k = k_ref[b, rs, :]
            vb = v_ref[b, rs, :]
            v = vb.astype(F32)
            bloc = bloc_all[rs]
            bend = bloc[0:1] if rev else bloc[SUB - 1:SUB]
            qk_terms = []
            for s in range(SUB):
                valid = (t_idx <= s) if rev else (t_idx >= s)
                rel = jnp.where(valid, bloc - bloc[s:s + 1], -jnp.inf)
                qk_terms.append((q * k[s:s + 1] * jnp.exp2(rel)).astype(BF16))
            att = _dot(jnp.concatenate(qk_terms, axis=0), expand)
            o_diag = att[0:SUB] * v[0:1]
            for s in range(1, SUB):
                o_diag = o_diag + att[s * SUB:(s + 1) * SUB] * v[s:s + 1]
            upd = _dot_tn(vb, (k * jnp.exp2(bend - bloc)).astype(BF16))
            parts[j] = ((q * jnp.exp2(bloc)).astype(BF16), o_diag, jnp.exp2(bend), jnp.where(same_head, upd, 0.0))
        st = st_ref[2 * b + rev]
        for j in order:
            q_dec, o_diag, decay, upd = parts[j]
            o_ref[b, j * SUB:(j + 1) * SUB, :] = _dot_nt(q_dec, st.astype(BF16)) + o_diag
            st = st * decay + upd
        st_ref[2 * b + rev] = st


def _gla(qg, kg, vg, la, n_ctx, rows):
    bsz, seq, _ = qg.shape
    nc = n_ctx // rows
    nblk = seq // rows

    def fwd(i):
        return (0, i, 0)

    def bwd_blk(i):
        return jnp.where(i < nc, nc - 1 - i, nblk + nc - 1 - i)

    def bwd(i):
        return (0, bwd_blk(i), 0)

    def spec(w, imap):
        return pl.BlockSpec((bsz, rows, w), imap)

    return pl.pallas_call(
        functools.partial(_gla_kernel, bsz=bsz, rows=rows),
        grid=(nblk,),
        in_specs=[spec(GLA_K, fwd), spec(GLA_K, fwd), spec(GLA_V, fwd), spec(GLA_K, fwd),
                  spec(GLA_K, bwd), spec(GLA_K, bwd), spec(GLA_V, bwd),
                  spec(GLA_K, lambda i: (0, bwd_blk(i), 1))],
        out_specs=[spec(GLA_V, fwd), spec(GLA_V, bwd)],
        out_shape=[jax.ShapeDtypeStruct((bsz, seq, GLA_V), F32)] * 2,
        scratch_shapes=[pltpu.VMEM((2 * bsz, GLA_V, GLA_K), F32)],
        name="gla_scan",
        compiler_params=_cparams(("arbitrary",)),
    )(qg, kg, vg, la, qg, kg, vg, la)


def _attn_kernel(lq1_ref, lk1_ref, lq2_ref, lk2_ref, q_ref, k_ref, v_ref, o_ref,
                 m_ref, acc_ref, sa_ref, sb_ref, ma_ref, mb_ref, *, n_chunks, tk, lam_init):
    lam = (jnp.exp(jnp.sum(lq1_ref[...] * lk1_ref[...], axis=-1, keepdims=True))
           - jnp.exp(jnp.sum(lq2_ref[...] * lk2_ref[...], axis=-1, keepdims=True)) + lam_init)
    q = q_ref[0]
    lane = lax.broadcasted_iota(jnp.int32, (1, LANES), 1)
    zero = jnp.zeros_like(q)
    qs = (jnp.where(lane < DIFF_DH, q, zero), jnp.where(lane >= DIFF_DH, q, zero))
    m_ref[...] = jnp.full_like(m_ref, -jnp.inf)
    acc_ref[...] = jnp.zeros_like(acc_ref)

    def keys(j):
        return pl.ds(pl.multiple_of(j * tk, tk), tk)

    def scores(j, s_ref, mx_ref):
        k = k_ref[0, keys(j), :]
        for mp in range(2):
            s = _dot_nt(qs[mp], k)
            s_ref[mp] = s
            mx_ref[mp] = jnp.max(s, axis=-1, keepdims=True)

    def consume(j, s_ref, mx_ref):
        v_ext = jnp.concatenate([v_ref[0, keys(j), :], jnp.ones((tk, LANES), BF16)], axis=1)
        for mp in range(2):
            m_old = m_ref[mp]
            m_new = jnp.maximum(m_old, mx_ref[mp])
            p = jnp.exp2(s_ref[mp] - m_new).astype(BF16)
            acc_ref[mp] = jnp.exp2(m_old - m_new) * acc_ref[mp] + _dot(p, v_ext)
            m_ref[mp] = m_new

    buf_a, buf_b = (sa_ref, ma_ref), (sb_ref, mb_ref)
    scores(0, *buf_a)

    def chunk_pair(jj, carry):
        scores(2 * jj + 1, *buf_b)
        consume(2 * jj, *buf_a)
        scores(2 * jj + 2, *buf_a)
        consume(2 * jj + 1, *buf_b)
        return carry

    lax.fori_loop(0, n_chunks // 2, chunk_pair, 0)
    consume(n_chunks - 1, *buf_a)
    o_ref[0] = (acc_ref[0, :, 0:LANES] / acc_ref[0, :, LANES:]
                - lam * (acc_ref[1, :, 0:LANES] / acc_ref[1, :, LANES:]))


def _key_chunk(n_kv, target):
    sizes = [c for c in range(LANES, n_kv + 1, LANES) if n_kv % c == 0 and (n_kv // c) % 2 == 1]
    return max([c for c in sizes if c <= target] or sizes[:1])


def _attention(qd, kd, vd, lam_vecs, lam_init, n_q, n_kv, tq):
    bsz = qd.shape[0]
    tk = _key_chunk(n_kv, KEY_CHUNK_TARGET)
    n_chunks = n_kv // tk
    assert n_chunks % 2 == 1 and n_q % tq == 0
    lam_spec = pl.BlockSpec((1, DIFF_DH), lambda b, h, i: (0, 0))
    return pl.pallas_call(
        functools.partial(_attn_kernel, n_chunks=n_chunks, tk=tk, lam_init=lam_init),
        grid=(bsz, DIFF_HEADS, n_q // tq),
        in_specs=[lam_spec] * 4 + [
            pl.BlockSpec((1, tq, LANES), lambda b, h, i: (b, i, h)),
            pl.BlockSpec((1, n_kv, LANES), lambda b, h, i: (b, 0, h)),
            pl.BlockSpec((1, n_kv, LANES), lambda b, h, i: (b, 0, h))],
        out_specs=pl.BlockSpec((1, tq, LANES), lambda b, h, i: (b, i, h)),
        out_shape=jax.ShapeDtypeStruct((bsz, n_q, DIFF_V), F32),
        scratch_shapes=[pltpu.VMEM((2, tq, 1), F32), pltpu.VMEM((2, tq, 2 * LANES), F32),
                        pltpu.VMEM((2, tq, tk), F32), pltpu.VMEM((2, tq, tk), F32),
                        pltpu.VMEM((2, tq, 1), F32), pltpu.VMEM((2, tq, 1), F32)],
        name="diff_attn",
        compiler_params=_cparams(("parallel", "parallel", "arbitrary")),
    )(*lam_vecs, qd, kd, vd)


def _postmix_kernel(x_ref, of_ref, ob_ref, og_ref, odc_ref, odl_ref, pc_ref, pp_ref, pn_ref,
                    gn_ref, dn_ref, pw_ref, ps_ref, wo_ref, g1_ref, sh2_ref, sc2_ref, n2_ref,
                    wr_ref, br_ref, x1_ref, h2_ref, route_ref, ext_ref,
                    *, tm, nct, n_ctx, seq, lam_init):
    i = pl.program_id(1)

    a = of_ref[0] + ob_ref[0]
    avg = ((lax.broadcasted_iota(jnp.int32, (GLA_V, GLA_V), 0) // GLA_DV
            == lax.broadcasted_iota(jnp.int32, (GLA_V, GLA_V), 1) // GLA_DV).astype(F32) * (1.0 / GLA_DV)).astype(BF16)
    sq_hi, sq_lo = _split_bf16(a * a)
    gla = a * lax.rsqrt(_dot(sq_hi, avg) + _dot(sq_lo, avg) + EPS) * gn_ref[...] * _silu(og_ref[0])
    y = _dot(gla.astype(BF16), wo_ref[0:GLA_V, :])

    for hd in range(DIFF_HEADS):
        lo = hd * DIFF_DV
        od = jnp.where(i < nct, odc_ref[0, :, lo:lo + DIFF_DV], odl_ref[0, :, lo:lo + DIFF_DV])
        dh = _rms(od) * dn_ref[:, lo:lo + DIFF_DV] * (1.0 - lam_init)
        y = y + _dot(dh.astype(BF16), wo_ref[GLA_V + lo:GLA_V + lo + DIFF_DV, :])

    seg_lo = jnp.where(i < nct, 0, n_ctx)
    seg_hi = jnp.where(i < nct, n_ctx, seq)
    ext_ref[0:POOL_HALO] = pp_ref[0]
    ext_ref[POOL_HALO:POOL_HALO + tm] = pc_ref[0]
    ext_ref[POOL_HALO + tm:] = pn_ref[0]
    pos_e = i * tm - POOL_HALO + lax.broadcasted_iota(jnp.int32, (tm + 2 * POOL_HALO, 1), 0)
    e = jnp.where((pos_e >= seg_lo) & (pos_e < seg_hi), ext_ref[...], 0.0)
    pos = i * tm + lax.broadcasted_iota(jnp.int32, (tm, 1), 0)
    grp = lax.broadcasted_iota(jnp.int32, (1, POOL_W), 1) // POOL_CH
    run, width, mean = e, 1, jnp.zeros((tm, POOL_W), F32)
    for gi, w in enumerate(POOL_WINDOWS):
        while width < w:
            n = run.shape[0] - width
            run = run[0:n] + run[width:width + n]
            width *= 2
        start = POOL_HALO - w // 2
        cnt = (jnp.minimum(pos + (w - w // 2), seg_hi) - jnp.maximum(pos - w // 2, seg_lo)).astype(F32)
        mean = jnp.where(grp == gi, run[start:start + tm] / cnt, mean)
    pooled = _dot((mean - pc_ref[0]).astype(BF16), pw_ref[...]) * ps_ref[...]
    y = y + _dot(pooled.astype(BF16), wo_ref[GLA_V + DIFF_V:, :])

    x1 = x_ref[0] + g1_ref[0] * y
    x1_ref[0] = x1
    h2 = _rms(x1) * n2_ref[...] * (1.0 + sc2_ref[0]) + sh2_ref[0]
    bits = lax.bitcast_convert_type(h2.astype(BF16).astype(F32), jnp.uint32)
    half = bits.shape[1] // 2
    h2_ref[0] = (bits[:, 0:half] >> 16) | (bits[:, half:] & jnp.uint32(HI16))

    h_hi, h_lo = _split_bf16(h2)
    both = _dot(h_hi, wr_ref[...])
    logit = both[:, 0:ROUTE_W] + both[:, ROUTE_W:] + _dot(h_lo, wr_ref[:, 0:ROUTE_W]) + br_ref[...]
    lane = lax.broadcasted_iota(jnp.int32, (1, ROUTE_W), 1).astype(F32)
    neg = -jnp.inf

    def top(vals):
        mx = jnp.max(vals, axis=-1, keepdims=True)
        idx = jnp.min(jnp.where(vals == mx, lane, float(ROUTE_W)), axis=-1, keepdims=True)
        return mx, idx

    gl = jnp.where(lane < N_GROUPS, logit, neg)
    gmax, gidx = top(gl)
    g_top = 1.0 / jnp.sum(jnp.exp(gl - gmax), axis=-1, keepdims=True)
    e_lo = N_GROUPS + gidx * EXPERTS_PER_GROUP
    el = jnp.where((lane >= e_lo) & (lane < e_lo + EXPERTS_PER_GROUP), logit, neg)
    emax, idx1 = top(el)
    esum = jnp.sum(jnp.exp(el - emax), axis=-1, keepdims=True)
    emax2, idx2 = top(jnp.where(lane == idx1, neg, el))
    e1 = 1.0 / esum
    e2 = jnp.exp(emax2 - emax) / esum
    w1 = g_top * e1 / (e1 + e2)
    w2 = g_top * e2 / (e1 + e2)
    rec = jnp.where(lane == 0, idx1 - N_GROUPS, 0.0)
    rec = jnp.where(lane == 1, idx2 - N_GROUPS, rec)
    rec = jnp.where(lane == 2, w1, rec)
    route_ref[0] = jnp.where(lane == 3, w2, rec)


def _postmix(xall, o_f, o_b, og, od_ctx, od_lat, pool, mod, prm, nct, n_ctx, tm, lam_init):
    bsz, seq, d = xall.shape
    hpb = tm // POOL_HALO
    n_halo = seq // POOL_HALO

    def mod_spec(col):
        return pl.BlockSpec((1, 1, d), lambda b, i: (jnp.where(i < nct, bsz, b), 0, col))

    tile = lambda w: pl.BlockSpec((1, tm, w), lambda b, i: (b, i, 0))
    full = lambda r, c: pl.BlockSpec((r, c), lambda b, i: (0, 0))
    return pl.pallas_call(
        functools.partial(_postmix_kernel, tm=tm, nct=nct, n_ctx=n_ctx, seq=seq, lam_init=lam_init),
        grid=(bsz, seq // tm),
        in_specs=[tile(d), tile(GLA_V), tile(GLA_V), tile(GLA_V),
                  pl.BlockSpec((1, tm, DIFF_V), lambda b, i: (b, jnp.minimum(i, nct - 1), 0)),
                  pl.BlockSpec((1, tm, DIFF_V), lambda b, i: (b, jnp.maximum(i - nct, 0), 0)),
                  tile(POOL_W),
                  pl.BlockSpec((1, POOL_HALO, POOL_W), lambda b, i: (b, jnp.maximum(i * hpb - 1, 0), 0)),
                  pl.BlockSpec((1, POOL_HALO, POOL_W), lambda b, i: (b, jnp.minimum((i + 1) * hpb, n_halo - 1), 0)),
                  full(1, GLA_V), full(1, DIFF_V), full(POOL_W, POOL_W), full(1, POOL_W), full(d, d),
                  mod_spec(2), mod_spec(3), mod_spec(4), full(1, d), full(d, 2 * ROUTE_W), full(1, ROUTE_W)],
        out_specs=[tile(d), tile(d // 2), tile(ROUTE_W)],
        out_shape=[jax.ShapeDtypeStruct((bsz, seq, d), F32), jax.ShapeDtypeStruct((bsz, seq, d // 2), jnp.uint32),
                   jax.ShapeDtypeStruct((bsz, seq, ROUTE_W), F32)],
        scratch_shapes=[pltpu.VMEM((tm + 2 * POOL_HALO, POOL_W), F32)],
        name="postmix",
        compiler_params=_cparams(("parallel", "parallel")),
    )(xall, o_f, o_b, og, od_ctx, od_lat, pool, pool, pool, prm["gla_norm"], prm["diff_norm"], prm["pool_w"],
      prm["pool_scale"], prm["w_out"], mod, mod, mod, prm["norm2"], prm["w_route"], prm["b_route"])


def _slot_of_assignment(expert, n_tok):
    n_assign = n_tok * TOP_K
    e = expert.reshape(n_assign)
    hot = (e[:, None] == jnp.arange(N_EXPERTS, dtype=jnp.int32)[None, :]).astype(jnp.int32)
    csum = jnp.cumsum(hot, axis=0)
    counts = csum[-1]
    rank = jnp.sum(csum * hot, axis=1) - 1
    padded = (counts + MOE_BLOCK - 1) // MOE_BLOCK * MOE_BLOCK
    padded_end = jnp.cumsum(padded)
    dest = (padded_end - padded)[e] + rank
    n_slots = -(-n_assign // MOE_BLOCK) * MOE_BLOCK + N_EXPERTS * MOE_BLOCK
    block_start = jnp.arange(n_slots // MOE_BLOCK, dtype=jnp.int32) * MOE_BLOCK
    block_expert = jnp.minimum(jnp.sum((padded_end[None, :] <= block_start[:, None]).astype(jnp.int32), axis=1),
                               N_EXPERTS - 1)
    return dest, block_expert, n_slots


def _each(n, fn):
    def step(r, carry):
        fn(r)
        return carry
    lax.fori_loop(0, n, step, 0, unroll=DMA_UNROLL)


def _dispatch_kernel(dest_ref, h_ref, xs_in, xs_hbm, buf, sem, *, tile):
    del xs_in
    i = pl.program_id(0)
    n_tiles = pl.num_programs(0) - 1

    def copy(r, k):
        return pltpu.make_async_copy(buf.at[i & 1, pl.ds(r, 1), :],
                                     xs_hbm.at[pl.ds(dest_ref[0, 0, r * TOP_K + k], 1), :], sem.at[i & 1])

    @pl.when(i < n_tiles)
    def _():
        buf[i & 1] = h_ref[...]
        _each(tile, lambda r: [copy(r, k).start() for k in range(TOP_K)])

    @pl.when(i > 0)
    def _():
        prev = (i - 1) & 1
        for _ in range(TOP_K):
            pltpu.make_async_copy(buf.at[prev], xs_hbm.at[pl.ds(0, tile), :], sem.at[prev]).wait()


def _dispatch(h2p, dest, n_slots, tile):
    n_tok, w = h2p.shape
    n_tiles = n_tok // tile
    dest3 = dest.reshape(n_tiles, 1, tile * TOP_K)
    smem = lambda imap: pl.BlockSpec((1, 1, tile * TOP_K), imap, memory_space=pltpu.SMEM)
    return pl.pallas_call(
        functools.partial(_dispatch_kernel, tile=tile),
        grid=(n_tiles + 1,),
        in_specs=[smem(lambda i: (jnp.minimum(i, n_tiles - 1), 0, 0)),
                  pl.BlockSpec((tile, w), lambda i: (jnp.minimum(i, n_tiles - 1), 0)),
                  pl.BlockSpec(memory_space=pl.ANY)],
        out_specs=pl.BlockSpec(memory_space=pl.ANY),
        out_shape=jax.ShapeDtypeStruct((n_slots, w), jnp.uint32),
        scratch_shapes=[pltpu.VMEM((2, tile, w), jnp.uint32), pltpu.SemaphoreType.DMA((2,))],
        input_output_aliases={2: 0},
        name="moe_dispatch",
        compiler_params=_cparams(("arbitrary",)),
    )(dest3, h2p, jnp.zeros((n_slots, w), jnp.uint32))


def _expert_kernel(be_ref, xs_ref, w1_ref, w3_ref, w2_ref, ys_ref, w1b_ref, w3b_ref, w2b_ref):
    i = pl.program_id(0)

    @pl.when((i == 0) | (be_ref[i] != be_ref[jnp.maximum(i - 1, 0)]))
    def _():
        w1b_ref[...] = w1_ref[0, 0].astype(BF16)
        w3b_ref[...] = w3_ref[0, 0].astype(BF16)
        w2b_ref[...] = w2_ref[0, 0].astype(BF16)

    bits = xs_ref[...]
    half = bits.shape[1]
    x_lo = lax.bitcast_convert_type(bits << 16, F32).astype(BF16)
    x_hi = lax.bitcast_convert_type(bits & jnp.uint32(HI16), F32).astype(BF16)

    def up(w_ref):
        return _dot(x_lo, w_ref[0:half, :]) + _dot(x_hi, w_ref[half:, :])

    ys_ref[...] = _dot((_silu(up(w1b_ref)) * up(w3b_ref)).astype(BF16), w2b_ref[...])


def _experts(xs, block_expert, w1, w3, w2, layer):
    n_slots, half = xs.shape
    _, _, d, d_exp = w1.shape
    return pl.pallas_call(
        _expert_kernel,
        grid_spec=pltpu.PrefetchScalarGridSpec(
            num_scalar_prefetch=1,
            grid=(n_slots // MOE_BLOCK,),
            in_specs=[pl.BlockSpec((MOE_BLOCK, half), lambda i, be: (i, 0)),
                      pl.BlockSpec((1, 1, d, d_exp), lambda i, be: (layer, be[i], 0, 0)),
                      pl.BlockSpec((1, 1, d, d_exp), lambda i, be: (layer, be[i], 0, 0)),
                      pl.BlockSpec((1, 1, d_exp, d), lambda i, be: (layer, be[i], 0, 0))],
            out_specs=pl.BlockSpec((MOE_BLOCK, d), lambda i, be: (i, 0)),
            scratch_shapes=[pltpu.VMEM((d, d_exp), BF16), pltpu.VMEM((d, d_exp), BF16),
                            pltpu.VMEM((d_exp, d), BF16)]),
        out_shape=jax.ShapeDtypeStruct((n_slots, d), F32),
        name="moe_experts",
        compiler_params=_cparams(("arbitrary",)),
    )(block_expert, xs, w1, w3, w2)


def _combine_kernel(dcur_ref, dnext_ref, x1_ref, route_ref, g2_ref, fn_ref, ys_hbm, o_ref, ybuf, sem,
                    *, final, tile):
    i = pl.program_id(0)
    n = pl.num_programs(0)

    def fetch(step, dest_ref, r, k):
        slot = step & 1
        return pltpu.make_async_copy(ys_hbm.at[pl.ds(dest_ref[0, 0, r * TOP_K + k], 1), :],
                                     ybuf.at[slot, k, pl.ds(r, 1), :], sem.at[slot])

    @pl.when(i == 0)
    def _():
        _each(tile, lambda r: [fetch(i, dcur_ref, r, k).start() for k in range(TOP_K)])

    @pl.when(i + 1 < n)
    def _():
        _each(tile, lambda r: [fetch(i + 1, dnext_ref, r, k).start() for k in range(TOP_K)])

    for k in range(TOP_K):
        pltpu.make_async_copy(ys_hbm.at[pl.ds(0, tile), :], ybuf.at[i & 1, k], sem.at[i & 1]).wait()
    route = route_ref[0]
    lane = lax.broadcasted_iota(jnp.int32, (1, ROUTE_W), 1)
    w0 = jnp.sum(jnp.where(lane == 2, route, 0.0), axis=-1, keepdims=True)
    w1 = jnp.sum(jnp.where(lane == 3, route, 0.0), axis=-1, keepdims=True)
    x = x1_ref[0] + g2_ref[0] * (ybuf[i & 1, 0] * w0 + ybuf[i & 1, 1] * w1)
    o_ref[0] = _rms(x) * fn_ref[...] if final else x


def _combine(x1, ys, dest, route, mod, final_norm, nct, tm, final):
    bsz, seq, d = x1.shape
    off = nct if final else 0
    tpb = seq // tm - off
    per_b = seq // tm

    def tok_tile(i):
        return (i // tpb) * per_b + off + i % tpb

    n_steps = bsz * tpb
    dest3 = dest.reshape(bsz * per_b, 1, tm * TOP_K)
    smem = lambda imap: pl.BlockSpec((1, 1, tm * TOP_K), imap, memory_space=pltpu.SMEM)
    tile = lambda w: pl.BlockSpec((1, tm, w), lambda i: (i // tpb, off + i % tpb, 0))
    return pl.pallas_call(
        functools.partial(_combine_kernel, final=final, tile=tm),
        grid=(n_steps,),
        in_specs=[smem(lambda i: (tok_tile(i), 0, 0)),
                  smem(lambda i: (tok_tile(jnp.minimum(i + 1, n_steps - 1)), 0, 0)),
                  tile(d), tile(ROUTE_W),
                  pl.BlockSpec((1, 1, d), lambda i: (jnp.where(off + i % tpb < nct, bsz, i // tpb), 0, 5)),
                  pl.BlockSpec((1, d), lambda i: (0, 0)),
                  pl.BlockSpec(memory_space=pl.ANY)],
        out_specs=pl.BlockSpec((1, tm, d), lambda i: (i // tpb, i % tpb, 0)),
        out_shape=jax.ShapeDtypeStruct((bsz, tpb * tm, d), F32),
        scratch_shapes=[pltpu.VMEM((2, TOP_K, tm, d), F32), pltpu.SemaphoreType.DMA((2,))],
        name="combine",
        compiler_params=_cparams(("arbitrary",)),
    )(dest3, dest3, x1, route, mod, final_norm, ys)


def _rope_tables(n_ctx, n_lat):
    t = jnp.arange(n_lat, dtype=jnp.int32)
    row = (t // GRID_W).astype(F32)
    col = (t % GRID_W).astype(F32)
    inv = 1.0 / (ROPE_BASE ** (jnp.arange(0, AX_DIM, 2, dtype=F32) / AX_DIM))
    lane = jnp.arange(LANES)
    within = lane % DIFF_DH
    pos = jnp.where((within < AX_DIM)[None, :], row[:, None], col[:, None])
    ang = pos * inv[within % (AX_DIM // 2)][None, :]
    sign = jnp.where((within % AX_DIM) < AX_DIM // 2, -1.0, 1.0)[None, :]
    cos = jnp.concatenate([jnp.ones((n_ctx, LANES), F32), jnp.cos(ang)], axis=0)
    sin = jnp.concatenate([jnp.zeros((n_ctx, LANES), F32), jnp.sin(ang) * sign], axis=0)
    return cos, sin


def _pack_layer(layer, w_in, w_out, wa2_f, ba_f, wa2_b, ba_b, pool_w, wg, bg, we, be):
    d = w_in.shape[1]
    wi = w_in[layer]
    o = 0
    parts = {}
    for name, size in (("qg", GLA_K), ("kg", GLA_K), ("vg", GLA_V), ("og", GLA_V), ("af", GATE_RANK),
                       ("ab", GATE_RANK), ("qd", DIFF_QK), ("kd", DIFF_QK), ("vd", DIFF_V), ("pl", POOL_W)):
        parts[name] = wi[:, o:o + size]
        o += size
    gate = jnp.concatenate([parts["af"], parts["ab"], jnp.zeros((d, LANES - 2 * GATE_RANK), F32)], axis=1)
    w_all = jnp.concatenate([parts[n] for n in ("qg", "kg", "vg", "og", "qd", "kd", "vd", "pl")] + [gate],
                            axis=1).astype(BF16)
    wa2 = jnp.zeros((LANES, 2 * GLA_K), F32)
    wa2 = wa2.at[0:GATE_RANK, 0:GLA_K].set(wa2_f[layer])
    wa2 = wa2.at[GATE_RANK:2 * GATE_RANK, GLA_K:].set(wa2_b[layer])
    ba = jnp.concatenate([ba_f[layer], ba_b[layer]])[None, :]
    pw = jnp.zeros((POOL_W, POOL_W), F32)
    for gi in range(len(POOL_WINDOWS)):
        pw = pw.at[gi * POOL_CH:(gi + 1) * POOL_CH, gi * POOL_CH:(gi + 1) * POOL_CH].set(pool_w[layer, gi])
    w_route = jnp.concatenate([wg[layer], we[layer], jnp.zeros((d, ROUTE_W - N_GROUPS - N_EXPERTS), F32)], axis=1)
    b_route = jnp.concatenate([bg[layer], be[layer], jnp.zeros((ROUTE_W - N_GROUPS - N_EXPERTS,), F32)])[None, :]
    head = lax.bitcast_convert_type(lax.bitcast_convert_type(w_route, jnp.uint32) & jnp.uint32(HI16), F32)
    w_route = jnp.concatenate([head.astype(BF16), (w_route - head).astype(BF16)], axis=1)
    return dict(w_all=w_all, wa2=wa2, ba=ba, pool_w=pw.astype(BF16), w_out=w_out[layer].astype(BF16),
                w_route=w_route, b_route=b_route)


def kernel(x, c, ctx, c_ctx, w_mod, b_mod, norm1, norm2, w_in, w_out, gla_wa2_f, gla_ba_f, gla_wa2_b, gla_ba_b, gla_norm, lam_q1, lam_k1, lam_q2, lam_k2, diff_norm, pool_w, pool_scale, router_wg, router_bg, router_we, router_be, exp_w1, exp_w3, exp_w2, final_norm):
    bsz, n_lat, d = x.shape
    n_ctx = ctx.shape[1]
    depth = w_mod.shape[0]
    seq = n_ctx + n_lat
    tm = math.gcd(TOKEN_TILE, n_ctx)
    nct = n_ctx // tm
    tq = math.gcd(QUERY_TILE, n_lat)
    gla_rows = math.gcd(GLA_ROWS, n_ctx)
    assert bsz + 1 <= SUBLANES and n_lat % tm == 0 and n_lat % GRID_W == 0

    cond = jnp.concatenate([c, c_ctx[None, :], jnp.zeros((SUBLANES - bsz - 1, d), F32)], axis=0)
    mod_all = _adaln(cond, w_mod, b_mod)
    cos_t, sin_t = _rope_tables(n_ctx, n_lat)
    xall = jnp.concatenate([ctx, x], axis=1)
    n_tok = bsz * seq

    for layer in range(depth):
        last = layer == depth - 1
        lam_init = 0.8 - 0.6 * math.exp(-0.3 * layer)
        prm = _pack_layer(layer, w_in, w_out, gla_wa2_f, gla_ba_f, gla_wa2_b, gla_ba_b, pool_w,
                          router_wg, router_bg, router_we, router_be)
        prm.update(gla_norm=gla_norm[layer][None, :], diff_norm=diff_norm[layer][None, :],
                   pool_scale=pool_scale[layer][None, :], norm2=norm2[layer][None, :])
        mod = mod_all[layer].reshape(SUBLANES, 1, 6 * d)

        qg, kg, vg, og, la, qd, kd, vd, pool = _premix(
            xall, mod, norm1[layer][None, :], prm["w_all"], prm["wa2"], prm["ba"], cos_t, sin_t, nct, tm)
        o_f, o_b = _gla(qg, kg, vg, la, n_ctx, gla_rows)
        lam_vecs = [v[layer][None, :] for v in (lam_q1, lam_k1, lam_q2, lam_k2)]
        od_lat = _attention(qd[:, n_ctx:], kd, vd, lam_vecs, lam_init, n_lat, seq, tq)
        od_ctx = _attention(qd, kd, vd, lam_vecs, lam_init, n_ctx, n_ctx, tm)
        x1, h2p, route = _postmix(xall, o_f, o_b, og, od_ctx, od_lat, pool, mod, prm, nct, n_ctx, tm, lam_init)

        expert = route[..., 0:TOP_K].astype(jnp.int32).reshape(n_tok, TOP_K)
        dest, block_expert, n_slots = _slot_of_assignment(expert, n_tok)
        xs = _dispatch(h2p.reshape(n_tok, d // 2), dest, n_slots, tm)
        ys = _experts(xs, block_expert, exp_w1, exp_w3, exp_w2, layer)
        xall = _combine(x1, ys, dest, route, mod, final_norm[None, :], nct, tm, last)
    return xall
```
